```python
import math
import jax, jax.numpy as jnp
from jax import lax
import numpy as np

D_MODEL = 2048
BATCH = 4
SEQ = 4096
DEPTH = 2

CHUNK = 64
N_MIXERS = 2
N_ATTN_LAYERS = (DEPTH + 1) // 2
N_LRU_LAYERS = DEPTH // 2
DIFF_HEAD_DIM = 128
DIFF_HEADS = D_MODEL // (2 * DIFF_HEAD_DIM)
Q_BLOCK = 128
D_RNN = D_MODEL * 5 // 4
LRU_BLOCK_W = 256
LRU_BLOCKS = D_RNN // LRU_BLOCK_W
CONV_WIDTH = 4
LRU_C = 8.0
D_FF = D_MODEL * 11 // 4
ALPHA = (2 * DEPTH) ** 0.25
BETA = (8 * DEPTH) ** -0.25
LN_EPS = 1e-5
RMS_EPS = 1e-5

kernel_name = "hybrid_diffattn_rglru_macaron_deepnorm_adaln"


def layer_norm(x, g, b):
    x32 = x.astype(jnp.float32)
    mu = jnp.mean(x32, axis=-1, keepdims=True)
    var = jnp.mean(jnp.square(x32 - mu), axis=-1, keepdims=True)
    y = (x32 - mu) * lax.rsqrt(var + LN_EPS) * g.astype(jnp.float32) + b.astype(jnp.float32)
    return y.astype(x.dtype)


def modulate(x, shift, scale):
    return x * (1 + scale[:, None, :]) + shift[:, None, :]


def swiglu(h, w_in, w_out):
    a, u = jnp.split(h @ w_in, 2, axis=-1)
    return (jax.nn.silu(a) * u) @ w_out


def diff_attention(h, w_qkv, w_o, lq1, lk1, lq2, lk2, subln_g, lambda_init):
    B, S, _ = h.shape
    H, d = DIFF_HEADS, DIFF_HEAD_DIM
    q, k, v = jnp.split(h @ w_qkv, 3, axis=-1)
    q = q.reshape(B, S, H, 2, d)
    k = k.reshape(B, S, H, 2, d)
    v = v.reshape(B, S, H, 2 * d)
    lam = (jnp.exp(jnp.sum(lq1.astype(jnp.float32) * lk1.astype(jnp.float32)))
           - jnp.exp(jnp.sum(lq2.astype(jnp.float32) * lk2.astype(jnp.float32)))
           + lambda_init)
    scale = d ** -0.5
    k_chunk = jnp.arange(S) // CHUNK
    n_qb = S // Q_BLOCK
    q_blocks = q.reshape(B, n_qb, Q_BLOCK, H, 2, d).transpose(1, 0, 2, 3, 4, 5)
    starts = jnp.arange(n_qb, dtype=jnp.int32) * Q_BLOCK

    def one_block(args):
        q_blk, start = args
        s = jnp.einsum('bqhjd,bkhjd->bhjqk', q_blk, k).astype(jnp.float32) * scale
        q_chunk = (start + jnp.arange(Q_BLOCK)) // CHUNK
        mask = k_chunk[None, :] <= q_chunk[:, None]
        s = jnp.where(mask, s, -jnp.inf)
        p = jax.nn.softmax(s, axis=-1)
        a = p[:, :, 0] - lam * p[:, :, 1]
        return jnp.einsum('bhqk,bkhe->bqhe', a.astype(v.dtype), v)

    o = lax.map(one_block, (q_blocks, starts))
    o = o.transpose(1, 0, 2, 3, 4).reshape(B, S, H, 2 * d)
    o32 = o.astype(jnp.float32)
    o32 = o32 * lax.rsqrt(jnp.mean(jnp.square(o32), axis=-1, keepdims=True) + RMS_EPS)
    o32 = o32 * subln_g.astype(jnp.float32) * (1.0 - lambda_init)
    return o32.astype(h.dtype).reshape(B, S, H * 2 * d) @ w_o


def rglru_block(h, w_in, conv_w, conv_b, ga_w, ga_b, gx_w, gx_b, lam_param, w_out):
    B, S, _ = h.shape
    g_branch, x_branch = jnp.split(h @ w_in, 2, axis=-1)
    g_branch = jax.nn.gelu(g_branch)
    xc = lax.conv_general_dilated(
        x_branch, conv_w[:, None, :].astype(x_branch.dtype), window_strides=(1,),
        padding=[(CONV_WIDTH - 1, 0)], dimension_numbers=('NWC', 'WIO', 'NWC'),
        feature_group_count=D_RNN) + conv_b
    xb = xc.reshape(B, S, LRU_BLOCKS, LRU_BLOCK_W)
    r = jax.nn.sigmoid(jnp.einsum('bsnc,ncd->bsnd', xb, ga_w).reshape(B, S, D_RNN) + ga_b)
    i = jax.nn.sigmoid(jnp.einsum('bsnc,ncd->bsnd', xb, gx_w).reshape(B, S, D_RNN) + gx_b)
    log_a = -LRU_C * r.astype(jnp.float32) * jax.nn.softplus(-lam_param.astype(jnp.float32))
    a = jnp.exp(log_a)
    mult = jnp.sqrt(jnp.maximum(-jnp.expm1(2.0 * log_a), 0.0))
    bx = xc.astype(jnp.float32) * i.astype(jnp.float32) * mult

    def combine(e1, e2):
        a1, b1 = e1
        a2, b2 = e2
        return a1 * a2, a2 * b1 + b2

    _, hs = lax.associative_scan(combine, (a, bx), axis=1)
    return (hs.astype(h.dtype) * g_branch) @ w_out


def setup_inputs(seed: int = 0) -> dict:
    key = jax.random.key(seed)
    ks = jax.random.split(key, 24)
    nrm = jax.random.normal
    f32 = jnp.float32
    x = nrm(ks[0], (BATCH, SEQ, D_MODEL), f32)
    c = nrm(ks[1], (BATCH, D_MODEL), f32)
    ada_w = nrm(ks[2], (DEPTH, D_MODEL, 9 * D_MODEL), f32) * (0.1 * D_MODEL ** -0.5)
    ada_b = nrm(ks[3], (DEPTH, 9 * D_MODEL), f32) * 0.01
    ln_g = 1.0 + 0.02 * nrm(ks[4], (DEPTH, 3, D_MODEL), f32)
    ln_b = 0.02 * nrm(ks[5], (DEPTH, 3, D_MODEL), f32)
    ffn_w_in = nrm(ks[6], (DEPTH, 2, D_MODEL, 2 * D_FF), f32) * D_MODEL ** -0.5
    ffn_w_out = nrm(ks[7], (DEPTH, 2, D_FF, D_MODEL), f32) * (D_FF ** -0.5 * BETA)
    attn_w_qkv = nrm(ks[8], (N_ATTN_LAYERS, D_MODEL, 3 * D_MODEL), f32) * D_MODEL ** -0.5
    attn_w_o = nrm(ks[9], (N_ATTN_LAYERS, D_MODEL, D_MODEL), f32) * (D_MODEL ** -0.5 * BETA)
    attn_lambda_q1 = 0.1 * nrm(ks[10], (N_ATTN_LAYERS, DIFF_HEAD_DIM), f32)
    attn_lambda_k1 = 0.1 * nrm(ks[11], (N_ATTN_LAYERS, DIFF_HEAD_DIM), f32)
    attn_lambda_q2 = 0.1 * nrm(ks[12], (N_ATTN_LAYERS, DIFF_HEAD_DIM), f32)
    attn_lambda_k2 = 0.1 * nrm(ks[13], (N_ATTN_LAYERS, DIFF_HEAD_DIM), f32)
    attn_subln_g = 1.0 + 0.02 * nrm(ks[14], (N_ATTN_LAYERS, 2 * DIFF_HEAD_DIM), f32)
    lru_w_in = nrm(ks[15], (N_LRU_LAYERS, D_MODEL, 2 * D_RNN), f32) * D_MODEL ** -0.5
    lru_conv_w = nrm(ks[16], (N_LRU_LAYERS, CONV_WIDTH, D_RNN), f32) * CONV_WIDTH ** -0.5
    lru_conv_b = 0.01 * nrm(ks[17], (N_LRU_LAYERS, D_RNN), f32)
    lru_gate_a_w = nrm(ks[18], (N_LRU_LAYERS, LRU_BLOCKS, LRU_BLOCK_W, LRU_BLOCK_W), f32) * LRU_BLOCK_W ** -0.5
    lru_gate_a_b = 0.01 * nrm(ks[19], (N_LRU_LAYERS, D_RNN), f32)
    lru_gate_x_w = nrm(ks[20], (N_LRU_LAYERS, LRU_BLOCKS, LRU_BLOCK_W, LRU_BLOCK_W), f32) * LRU_BLOCK_W ** -0.5
    lru_gate_x_b = 0.01 * nrm(ks[21], (N_LRU_LAYERS, D_RNN), f32)
    a_c = jax.random.uniform(ks[22], (N_LRU_LAYERS, D_RNN), f32, 0.9, 0.999)
    a_base = a_c ** (1.0 / LRU_C)
    lru_lambda = jnp.log(a_base) - jnp.log1p(-a_base)
    lru_w_out = nrm(ks[23], (N_LRU_LAYERS, D_RNN, D_MODEL), f32) * (D_RNN ** -0.5 * BETA)
    return {"x": x, "c": c, "ada_w": ada_w, "ada_b": ada_b, "ln_g": ln_g, "ln_b": ln_b,
            "ffn_w_in": ffn_w_in, "ffn_w_out": ffn_w_out,
            "attn_w_qkv": attn_w_qkv, "attn_w_o": attn_w_o,
            "attn_lambda_q1": attn_lambda_q1, "attn_lambda_k1": attn_lambda_k1,
            "attn_lambda_q2": attn_lambda_q2, "attn_lambda_k2": attn_lambda_k2,
            "attn_subln_g": attn_subln_g,
            "lru_w_in": lru_w_in, "lru_conv_w": lru_conv_w, "lru_conv_b": lru_conv_b,
            "lru_gate_a_w": lru_gate_a_w, "lru_gate_a_b": lru_gate_a_b,
            "lru_gate_x_w": lru_gate_x_w, "lru_gate_x_b": lru_gate_x_b,
            "lru_lambda": lru_lambda, "lru_w_out": lru_w_out}


def reference(x, c, ada_w, ada_b, ln_g, ln_b, ffn_w_in, ffn_w_out,
              attn_w_qkv, attn_w_o, attn_lambda_q1, attn_lambda_k1, attn_lambda_q2,
              attn_lambda_k2, attn_subln_g,
              lru_w_in, lru_conv_w, lru_conv_b, lru_gate_a_w, lru_gate_a_b,
              lru_gate_x_w, lru_gate_x_b, lru_lambda, lru_w_out):
    c_act = jax.nn.silu(c)
    for i in range(DEPTH):
        mod = c_act @ ada_w[i] + ada_b[i]
        sh1, sc1, g1, sh2, sc2, g2, sh3, sc3, g3 = jnp.split(mod, 9, axis=-1)
        y = swiglu(modulate(x, sh1, sc1), ffn_w_in[i, 0], ffn_w_out[i, 0])
        x = layer_norm(ALPHA * x + 0.5 * (1 + g1)[:, None, :] * y, ln_g[i, 0], ln_b[i, 0])
        h = modulate(x, sh2, sc2)
        j = i // N_MIXERS
        if i % N_MIXERS == 0:
            lambda_init = 0.8 - 0.6 * math.exp(-0.3 * i)
            y = diff_attention(h, attn_w_qkv[j], attn_w_o[j], attn_lambda_q1[j], attn_lambda_k1[j],
                               attn_lambda_q2[j], attn_lambda_k2[j], attn_subln_g[j], lambda_init)
        else:
            y = rglru_block(h, lru_w_in[j], lru_conv_w[j], lru_conv_b[j], lru_gate_a_w[j],
                            lru_gate_a_b[j], lru_gate_x_w[j], lru_gate_x_b[j], lru_lambda[j],
                            lru_w_out[j])
        x = layer_norm(ALPHA * x + (1 + g2)[:, None, :] * y, ln_g[i, 1], ln_b[i, 1])
        y = swiglu(modulate(x, sh3, sc3), ffn_w_in[i, 1], ffn_w_out[i, 1])
        x = layer_norm(ALPHA * x + 0.5 * (1 + g3)[:, None, :] * y, ln_g[i, 2], ln_b[i, 2])
    return x
```

```python
import functools
import math

import jax
import jax.numpy as jnp
from jax import lax
from jax.experimental import pallas as pl
from jax.experimental.pallas import tpu as pltpu

F32 = jnp.float32
BF16 = jnp.bfloat16

CHUNK = 64
DIFF_HEAD_DIM = 128
CONV_WIDTH = 4
LRU_BLOCK_W = 256
LRU_C = 8.0
LN_EPS = 1e-5
RMS_EPS = 1e-5
NEG_BIG = -1e30

VMEM_LIMIT_BYTES = 56 * 1024 * 1024


def _params(*sem):
    return pltpu.CompilerParams(dimension_semantics=sem, vmem_limit_bytes=VMEM_LIMIT_BYTES)


def _dot(a, b):
    return jnp.dot(a, b, preferred_element_type=F32)


def _modulate(x, shift_row, scale_row):
    return x * (1.0 + scale_row) + shift_row


def _residual_layer_norm(x, y, gate_row, res_w, alpha, ln_g, ln_b):
    z = alpha * x + (res_w * (1.0 + gate_row)) * y
    mu = jnp.mean(z, axis=-1, keepdims=True)
    zc = z - mu
    var = jnp.mean(zc * zc, axis=-1, keepdims=True)
    return zc * lax.rsqrt(var + LN_EPS) * ln_g + ln_b


def _adaln_kernel(c_ref, w_ref, b_ref, o_ref):
    c = c_ref[...]
    c_act = (c * jax.nn.sigmoid(c)).astype(BF16)
    o_ref[...] = _dot(c_act, w_ref[...].astype(BF16)) + b_ref[...]


def _adaln(c, ada_w, ada_b, *, tn=1024):
    depth, d, n = ada_w.shape
    b = c.shape[0]
    return pl.pallas_call(
        _adaln_kernel,
        grid=(depth, n // tn),
        in_specs=[
            pl.BlockSpec((b, d), lambda l, j: (0, 0)),
            pl.BlockSpec((None, d, tn), lambda l, j: (l, 0, j)),
            pl.BlockSpec((None, 1, tn), lambda l, j: (l, 0, j)),
        ],
        out_specs=pl.BlockSpec((None, b, tn), lambda l, j: (l, 0, j)),
        out_shape=jax.ShapeDtypeStruct((depth, b, n), F32),
        compiler_params=_params("parallel", "parallel"),
        name="adaln",
    )(c, ada_w, ada_b.reshape(depth, 1, n))


def _mod_spec(k, tiles_per_batch, d):
    return pl.BlockSpec((None, 1, d), lambda i, *_: ((i // tiles_per_batch) * 9 + k, 0, 0))


def _row_spec(d):
    return pl.BlockSpec((1, d), lambda *_: (0, 0))


def _ffn_kernel(x_ref, sh_ref, sc_ref, gt_ref, wa_ref, wu_ref, wo_ref, lng_ref, lnb_ref,
                o_ref, h_ref, acc_ref, *, alpha):
    j = pl.program_id(1)

    @pl.when(j == 0)
    def _():
        h_ref[...] = _modulate(x_ref[...], sh_ref[...], sc_ref[...]).astype(BF16)

    h = h_ref[...]
    a = _dot(h, wa_ref[...])
    u = _dot(h, wu_ref[...])
    g = (a * jax.nn.sigmoid(a) * u).astype(BF16)
    y = _dot(g, wo_ref[...])

    @pl.when(j == 0)
    def _():
        acc_ref[...] = y

    @pl.when(j > 0)
    def _():
        acc_ref[...] += y

    @pl.when(j == pl.num_programs(1) - 1)
    def _():
        o_ref[...] = _residual_layer_norm(x_ref[...], acc_ref[...], gt_ref[...], 0.5, alpha,
                                          lng_ref[...], lnb_ref[...])


def _ffn(x, mod3, k0, w_in, w_out, ln_g, ln_b, *, seq, alpha, tm=512, tf=512):
    t, d = x.shape
    d_ff = w_out.shape[0]
    nf = d_ff // tf
    tpb = seq // tm
    return pl.pallas_call(
        functools.partial(_ffn_kernel, alpha=alpha),
        grid=(t // tm, nf),
        in_specs=[
            pl.BlockSpec((tm, d), lambda i, j: (i, 0)),
            _mod_spec(k0, tpb, d), _mod_spec(k0 + 1, tpb, d), _mod_spec(k0 + 2, tpb, d),
            pl.BlockSpec((d, tf), lambda i, j: (0, j)),
            pl.BlockSpec((d, tf), lambda i, j: (0, j + nf)),
            pl.BlockSpec((tf, d), lambda i, j: (j, 0)),
            _row_spec(d), _row_spec(d),
        ],
        out_specs=pl.BlockSpec((tm, d), lambda i, j: (i, 0)),
        out_shape=jax.ShapeDtypeStruct((t, d), F32),
        scratch_shapes=[pltpu.VMEM((tm, d), BF16), pltpu.VMEM((tm, d), F32)],
        compiler_params=_params("parallel", "arbitrary"),
        name="ffn",
    )(x, mod3, mod3, mod3, w_in, w_in, w_out, ln_g.reshape(1, d), ln_b.reshape(1, d))


def _gelu_tanh(x):
    return 0.5 * x * (1.0 + jnp.tanh(math.sqrt(2.0 / math.pi) * (x + 0.044715 * (x * x * x))))


def _inproj_kernel(x_ref, sh_ref, sc_ref, w_ref, cs_ref, o_ref, h_ref, *, epilogue):
    @pl.when(pl.program_id(1) == 0)
    def _():
        h_ref[...] = _modulate(x_ref[...], sh_ref[...], sc_ref[...]).astype(BF16)

    y = _dot(h_ref[...], w_ref[...])
    if epilogue == "colscale":
        y = y * cs_ref[...]
    elif epilogue == "gelu":
        y = _gelu_tanh(y)
    o_ref[...] = y.astype(o_ref.dtype)


def _inproj(x, mod3, k0, w, col_scale, *, seq, n_out, col_off, epilogue, out_dtype, tm=512, tn=512):
    t, d = x.shape
    tpb = seq // tm
    off = col_off // tn
    return pl.pallas_call(
        functools.partial(_inproj_kernel, epilogue=epilogue),
        grid=(t // tm, n_out // tn),
        in_specs=[
            pl.BlockSpec((tm, d), lambda i, j: (i, 0)),
            _mod_spec(k0, tpb, d), _mod_spec(k0 + 1, tpb, d),
            pl.BlockSpec((d, tn), lambda i, j: (0, j + off)),
            pl.BlockSpec((1, tn), lambda i, j: (0, j)),
        ],
        out_specs=pl.BlockSpec((tm, tn), lambda i, j: (i, j)),
        out_shape=jax.ShapeDtypeStruct((t, n_out), out_dtype),
        scratch_shapes=[pltpu.VMEM((tm, d), BF16)],
        compiler_params=_params("parallel", "arbitrary"),
        name="inproj_" + epilogue,
    )(x, mod3, mod3, w, col_scale)


def _outproj_kernel(a_ref, w_ref, x_ref, gt_ref, lng_ref, lnb_ref, o_ref, *, alpha):
    y = _dot(a_ref[...], w_ref[...])
    o_ref[...] = _residual_layer_norm(x_ref[...], y, gt_ref[...], 1.0, alpha,
                                      lng_ref[...], lnb_ref[...])


def _outproj(a, w, x, mod3, k_gate, ln_g, ln_b, *, seq, alpha, tm=256):
    t, d = x.shape
    k = a.shape[1]
    tpb = seq // tm
    return pl.pallas_call(
        functools.partial(_outproj_kernel, alpha=alpha),
        grid=(t // tm,),
        in_specs=[
            pl.BlockSpec((tm, k), lambda i: (i, 0)),
            pl.BlockSpec((k, d), lambda i: (0, 0)),
            pl.BlockSpec((tm, d), lambda i: (i, 0)),
            _mod_spec(k_gate, tpb, d),
            _row_spec(d), _row_spec(d),
        ],
        out_specs=pl.BlockSpec((tm, d), lambda i: (i, 0)),
        out_shape=jax.ShapeDtypeStruct((t, d), F32),
        compiler_params=_params("parallel"),
        name="outproj",
    )(a, w, x, mod3, ln_g.reshape(1, d), ln_b.reshape(1, d))


def _attn_kernel(lq1_ref, lk1_ref, lq2_ref, lk2_ref, sg_ref, q_ref, k_ref, v_ref, o_ref,
                 *, tq, lambda_init):
    dh = DIFF_HEAD_DIM
    qi = pl.program_id(2)
    q1 = q_ref[:, 0:dh]
    q2 = q_ref[:, dh:2 * dh]
    nt = (((1,), (1,)), ((), ()))

    def scores(kt):
        off = pl.multiple_of(kt * tq, tq)
        k = k_ref[pl.ds(off, tq), :]
        v = v_ref[pl.ds(off, tq), :]
        s1 = lax.dot_general(q1, k[:, 0:dh], nt, preferred_element_type=F32)
        s2 = lax.dot_general(q2, k[:, dh:2 * dh], nt, preferred_element_type=F32)
        return s1, s2, v

    def update(s, v, m, l, acc):
        m_new = jnp.maximum(m, jnp.max(s, axis=-1, keepdims=True))
        corr = jnp.exp(m - m_new)
        p = jnp.exp(s - m_new)
        l_new = corr * l + jnp.sum(p, axis=-1, keepdims=True)
        acc_new = corr * acc + _dot(p.astype(BF16), v)
        return m_new, l_new, acc_new

    def body(kt, carry):
        m1, l1, a1, m2, l2, a2 = carry
        s1, s2, v = scores(kt)
        m1, l1, a1 = update(s1, v, m1, l1, a1)
        m2, l2, a2 = update(s2, v, m2, l2, a2)
        return m1, l1, a1, m2, l2, a2

    m0 = jnp.full((tq, 1), NEG_BIG, F32)
    l0 = jnp.zeros((tq, 1), F32)
    a0 = jnp.zeros((tq, 2 * dh), F32)
    carry = lax.fori_loop(0, qi, body, (m0, l0, a0, m0, l0, a0))

    m1, l1, a1, m2, l2, a2 = carry
    s1, s2, v = scores(qi)
    qc = lax.broadcasted_iota(jnp.int32, (tq, tq), 0) // CHUNK
    kc = lax.broadcasted_iota(jnp.int32, (tq, tq), 1) // CHUNK
    allowed = kc <= qc
    s1 = jnp.where(allowed, s1, NEG_BIG)
    s2 = jnp.where(allowed, s2, NEG_BIG)
    m1, l1, a1 = update(s1, v, m1, l1, a1)
    m2, l2, a2 = update(s2, v, m2, l2, a2)

    lam = (jnp.exp(jnp.sum(lq1_ref[...] * lk1_ref[...], axis=-1, keepdims=True))
           - jnp.exp(jnp.sum(lq2_ref[...] * lk2_ref[...], axis=-1, keepdims=True))
           + lambda_init)
    o = a1 / l1 - lam * (a2 / l2)
    o = o * lax.rsqrt(jnp.mean(o * o, axis=-1, keepdims=True) + RMS_EPS)
    o = o * sg_ref[...] * (1.0 - lambda_init)
    o_ref[...] = o.astype(o_ref.dtype)


def _diff_attention(qkv, lq1, lk1, lq2, lk2, subln_g, *, batch, seq, heads, lambda_init, tq=256):
    dv = 2 * DIFF_HEAD_DIM
    nq = seq // tq
    vec = pl.BlockSpec((1, DIFF_HEAD_DIM), lambda b, h, i: (0, 0))
    return pl.pallas_call(
        functools.partial(_attn_kernel, tq=tq, lambda_init=lambda_init),
        grid=(batch, heads, nq),
        in_specs=[
            vec, vec, vec, vec,
            pl.BlockSpec((1, dv), lambda b, h, i: (0, 0)),
            pl.BlockSpec((tq, dv), lambda b, h, i: (b * nq + i, h)),
            pl.BlockSpec((seq, dv), lambda b, h, i: (b, heads + h)),
            pl.BlockSpec((seq, dv), lambda b, h, i: (b, 2 * heads + h)),
        ],
        out_specs=pl.BlockSpec((tq, dv), lambda b, h, i: (b * nq + i, h)),
        out_shape=jax.ShapeDtypeStruct((batch * seq, heads * dv), BF16),
        compiler_params=_params("parallel", "parallel", "arbitrary"),
        name="diff_attn",
    )(lq1.reshape(1, -1), lk1.reshape(1, -1), lq2.reshape(1, -1), lk2.reshape(1, -1),
      subln_g.reshape(1, dv), qkv, qkv, qkv)


def _lru_kernel(x_ref, g_ref, cw_ref, cb_ref, gaw_ref, gab_ref, gxw_ref, gxb_ref, lam_ref,
                o_ref, xe_ref, hc_ref, a_s, b_s, h_s, *, tt):
    t = pl.program_id(2)
    w = x_ref.shape[1]

    @pl.when(t == 0)
    def _():
        xe_ref[0:8, :] = jnp.zeros((8, w), F32)
        hc_ref[...] = jnp.zeros_like(hc_ref)

    x0 = x_ref[...]
    xe_ref[8:8 + tt, :] = x0
    cw = cw_ref[...]
    xc = (cw[3:4, :] * x0 + cw[2:3, :] * xe_ref[7:7 + tt, :] + cw[1:2, :] * xe_ref[6:6 + tt, :]
          + cw[0:1, :] * xe_ref[5:5 + tt, :] + cb_ref[...])
    xe_ref[0:8, :] = x0[tt - 8:tt, :]

    xcb = xc.astype(BF16)
    r = jax.nn.sigmoid(_dot(xcb, gaw_ref[...]) + gab_ref[...])
    gi = jax.nn.sigmoid(_dot(xcb, gxw_ref[...]) + gxb_ref[...])
    z = -lam_ref[...]
    softplus = jnp.maximum(z, 0.0) + jnp.log(1.0 + jnp.exp(-jnp.abs(z)))
    log_a = (-LRU_C) * r * softplus
    a = jnp.exp(log_a)
    mult = jnp.sqrt(jnp.maximum(1.0 - a * a, 0.0))
    bx = xc * gi * mult

    rm = lax.broadcasted_iota(jnp.int32, (tt, w), 0) % 8
    for d in (1, 2, 4):
        a_sh = pltpu.roll(a, d, 0)
        b_sh = pltpu.roll(bx, d, 0)
        keep = rm >= d
        bx = jnp.where(keep, a * b_sh + bx, bx)
        a = jnp.where(keep, a * a_sh, a)
    a_s[...] = a
    b_s[...] = bx

    def body(g, carry):
        off = pl.multiple_of(g * 8, 8)
        h = a_s[pl.ds(off, 8), :] * carry + b_s[pl.ds(off, 8), :]
        h_s[pl.ds(off, 8), :] = h
        return h[7:8, :]

    carry = lax.fori_loop(0, tt // 8, body, hc_ref[0:1, :], unroll=8)
    hc_ref[0:1, :] = carry
    o_ref[...] = (h_s[...] * g_ref[...].astype(F32)).astype(o_ref.dtype)


def _lru_core(xb, gb, conv_w, conv_b, ga_w, ga_b, gx_w, gx_b, lam, *, batch, seq, tt=512):
    t, d_rnn = xb.shape
    bw = LRU_BLOCK_W
    nb = d_rnn // bw
    nt = seq // tt
    tile = lambda b, n, s: (b * nt + s, n)
    row = lambda b, n, s: (0, n)
    gate_w = lambda b, n, s: (n, 0, 0)
    return pl.pallas_call(
        functools.partial(_lru_kernel, tt=tt),
        grid=(batch, nb, nt),
        in_specs=[
            pl.BlockSpec((tt, bw), tile),
            pl.BlockSpec((tt, bw), tile),
            pl.BlockSpec((CONV_WIDTH, bw), row),
            pl.BlockSpec((1, bw), row),
            pl.BlockSpec((None, bw, bw), gate_w),
            pl.BlockSpec((1, bw), row),
            pl.BlockSpec((None, bw, bw), gate_w),
            pl.BlockSpec((1, bw), row),
            pl.BlockSpec((1, bw), row),
        ],
        out_specs=pl.BlockSpec((tt, bw), tile),
        out_shape=jax.ShapeDtypeStruct((t, d_rnn), BF16),
        scratch_shapes=[
            pltpu.VMEM((tt + 8, bw), F32),
            pltpu.VMEM((8, bw), F32),
            pltpu.VMEM((tt, bw), F32),
            pltpu.VMEM((tt, bw), F32),
            pltpu.VMEM((tt, bw), F32),
        ],
        compiler_params=_params("parallel", "parallel", "arbitrary"),
        name="lru_core",
    )(xb, gb, conv_w, conv_b.reshape(1, -1), ga_w, ga_b.reshape(1, -1), gx_w, gx_b.reshape(1, -1),
      lam.reshape(1, -1))


def kernel(x, c, ada_w, ada_b, ln_g, ln_b, ffn_w_in, ffn_w_out, attn_w_qkv, attn_w_o, attn_lambda_q1, attn_lambda_k1, attn_lambda_q2, attn_lambda_k2, attn_subln_g, lru_w_in, lru_conv_w, lru_conv_b, lru_gate_a_w, lru_gate_a_b, lru_gate_x_w, lru_gate_x_b, lru_lambda, lru_w_out):
    batch, seq, d = x.shape
    depth = ada_w.shape[0]
    d_rnn = lru_w_out.shape[1]
    heads = d // (2 * DIFF_HEAD_DIM)
    alpha = (2 * depth) ** 0.25
    n_mixers = 2

    mod = _adaln(c, ada_w, ada_b)
    xt = x.reshape(batch * seq, d)
    q_scale = jnp.concatenate([jnp.full((1, d), DIFF_HEAD_DIM ** -0.5, F32),
                               jnp.ones((1, 2 * d), F32)], axis=1)
    ones_row = jnp.ones((1, d_rnn), F32)

    for i in range(depth):
        mod3 = mod[i].reshape(batch * 9, 1, d)
        xt = _ffn(xt, mod3, 0, ffn_w_in[i, 0].astype(BF16), ffn_w_out[i, 0].astype(BF16),
                  ln_g[i, 0], ln_b[i, 0], seq=seq, alpha=alpha)
        j = i // n_mixers
        if i % n_mixers == 0:
            lambda_init = 0.8 - 0.6 * math.exp(-0.3 * i)
            qkv = _inproj(xt, mod3, 3, attn_w_qkv[j].astype(BF16), q_scale, seq=seq, n_out=3 * d,
                          col_off=0, epilogue="colscale", out_dtype=BF16)
            mixed = _diff_attention(qkv, attn_lambda_q1[j], attn_lambda_k1[j], attn_lambda_q2[j],
                                    attn_lambda_k2[j], attn_subln_g[j], batch=batch, seq=seq,
                                    heads=heads, lambda_init=lambda_init)
            w_o = attn_w_o[j].astype(BF16)
        else:
            w_in = lru_w_in[j].astype(BF16)
            gb = _inproj(xt, mod3, 3, w_in, ones_row, seq=seq, n_out=d_rnn, col_off=0,
                         epilogue="gelu", out_dtype=BF16)
            xb = _inproj(xt, mod3, 3, w_in, ones_row, seq=seq, n_out=d_rnn, col_off=d_rnn,
                         epilogue="none", out_dtype=F32)
            mixed = _lru_core(xb, gb, lru_conv_w[j], lru_conv_b[j], lru_gate_a_w[j].astype(BF16),
                              lru_gate_a_b[j], lru_gate_x_w[j].astype(BF16), lru_gate_x_b[j],
                              lru_lambda[j], batch=batch, seq=seq)
            w_o = lru_w_out[j].astype(BF16)
        xt = _outproj(mixed, w_o, xt, mod3, 5, ln_g[i, 1], ln_b[i, 1], seq=seq, alpha=alpha)
        xt = _ffn(xt, mod3, 6, ffn_w_in[i, 1].astype(BF16), ffn_w_out[i, 1].astype(BF16),
                  ln_g[i, 2], ln_b[i, 2], seq=seq, alpha=alpha)
    return xt.reshape(batch, seq, d)
```

```python
import functools
import math

import jax
import jax.numpy as jnp
from jax import lax
from jax.experimental import pallas as pl
from jax.experimental.pallas import tpu as pltpu

F32 = jnp.float32
BF16 = jnp.bfloat16

CHUNK = 64
DIFF_HEAD_DIM = 128
CONV_WIDTH = 4
LRU_BLOCK_W = 256
LRU_C = 8.0
LN_EPS = 1e-5
RMS_EPS = 1e-5
NEG_BIG = -1e30
OUTPROJ_ROW_BLOCK = 256
ATTN_ROW_BLOCK = 64
ATTN_STEPS_PER_TRIP = 4

VMEM_LIMIT_BYTES = 56 * 1024 * 1024


def _params(*sem):
    return pltpu.CompilerParams(dimension_semantics=sem, vmem_limit_bytes=VMEM_LIMIT_BYTES)


def _dot(a, b):
    return jnp.dot(a, b, preferred_element_type=F32)


def _modulate(x, shift_row, scale_row):
    return x * (1.0 + scale_row) + shift_row


def _residual_layer_norm(x, y, gate_row, res_w, alpha, ln_g, ln_b):
    z = alpha * x + (res_w * (1.0 + gate_row)) * y
    mu = jnp.mean(z, axis=-1, keepdims=True)
    zc = z - mu
    var = jnp.mean(zc * zc, axis=-1, keepdims=True)
    return zc * lax.rsqrt(var + LN_EPS) * ln_g + ln_b


def _adaln_kernel(c_ref, w_ref, b_ref, o_ref):
    c = c_ref[...]
    c_act = (c * jax.nn.sigmoid(c)).astype(BF16)
    o_ref[...] = _dot(c_act, w_ref[...].astype(BF16)) + b_ref[...]


def _adaln(c, ada_w, ada_b, *, tn=1024):
    depth, d, n = ada_w.shape
    b = c.shape[0]
    return pl.pallas_call(
        _adaln_kernel,
        grid=(depth, n // tn),
        in_specs=[
            pl.BlockSpec((b, d), lambda l, j: (0, 0)),
            pl.BlockSpec((None, d, tn), lambda l, j: (l, 0, j)),
            pl.BlockSpec((None, 1, tn), lambda l, j: (l, 0, j)),
        ],
        out_specs=pl.BlockSpec((None, b, tn), lambda l, j: (l, 0, j)),
        out_shape=jax.ShapeDtypeStruct((depth, b, n), F32),
        compiler_params=_params("parallel", "parallel"),
        name="adaln",
    )(c, ada_w, ada_b.reshape(depth, 1, n))


def _mod_spec(k, tiles_per_batch, d):
    return pl.BlockSpec((None, 1, d), lambda i, *_: ((i // tiles_per_batch) * 9 + k, 0, 0))


def _row_spec(d):
    return pl.BlockSpec((1, d), lambda *_: (0, 0))


def _ffn_kernel(x_ref, sh_ref, sc_ref, gt_ref, wa_ref, wu_ref, wo_ref, lng_ref, lnb_ref,
                o_ref, h_ref, acc_ref, *, alpha):
    j = pl.program_id(1)

    @pl.when(j == 0)
    def _():
        h_ref[...] = _modulate(x_ref[...], sh_ref[...], sc_ref[...]).astype(BF16)
        acc_ref[...] = jnp.zeros_like(acc_ref)

    h = h_ref[...]
    a = _dot(h, wa_ref[...])
    u = _dot(h, wu_ref[...])
    g = (a * jax.nn.sigmoid(a) * u).astype(BF16)
    acc_ref[...] += _dot(g, wo_ref[...])

    @pl.when(j == pl.num_programs(1) - 1)
    def _():
        o_ref[...] = _residual_layer_norm(x_ref[...], acc_ref[...], gt_ref[...], 0.5, alpha,
                                          lng_ref[...], lnb_ref[...])


def _ffn(x, mod3, k0, w_in, w_out, widx, ln_g, ln_b, *, seq, alpha, tm=512, tf=512):
    t, d = x.shape
    d_ff = w_out.shape[1]
    nf = d_ff // tf
    tpb = seq // tm
    return pl.pallas_call(
        functools.partial(_ffn_kernel, alpha=alpha),
        grid=(t // tm, nf),
        in_specs=[
            pl.BlockSpec((tm, d), lambda i, j: (i, 0)),
            _mod_spec(k0, tpb, d), _mod_spec(k0 + 1, tpb, d), _mod_spec(k0 + 2, tpb, d),
            pl.BlockSpec((None, d, tf), lambda i, j: (widx, 0, j)),
            pl.BlockSpec((None, d, tf), lambda i, j: (widx, 0, j + nf)),
            pl.BlockSpec((None, tf, d), lambda i, j: (widx, j, 0)),
            _row_spec(d), _row_spec(d),
        ],
        out_specs=pl.BlockSpec((tm, d), lambda i, j: (i, 0)),
        out_shape=jax.ShapeDtypeStruct((t, d), F32),
        scratch_shapes=[pltpu.VMEM((tm, d), BF16), pltpu.VMEM((tm, d), F32)],
        compiler_params=_params("parallel", "arbitrary"),
        name="ffn",
    )(x, mod3, mod3, mod3, w_in, w_in, w_out, ln_g.reshape(1, d), ln_b.reshape(1, d))


def _gelu_tanh(x):
    return 0.5 * x * (1.0 + jnp.tanh(math.sqrt(2.0 / math.pi) * (x + 0.044715 * (x * x * x))))


def _qkv_kernel(x_ref, sh_ref, sc_ref, w_ref, cs_ref, o_ref, h_ref):
    @pl.when(pl.program_id(1) == 0)
    def _():
        h_ref[...] = _modulate(x_ref[...], sh_ref[...], sc_ref[...]).astype(BF16)

    o_ref[...] = (_dot(h_ref[...], w_ref[...]) * cs_ref[...]).astype(o_ref.dtype)


def _qkv_proj(x, mod3, k0, w, col_scale, *, seq, tm=1024, tn=1536):
    t, d = x.shape
    n_out = w.shape[1]
    tpb = seq // tm
    return pl.pallas_call(
        _qkv_kernel,
        grid=(t // tm, n_out // tn),
        in_specs=[
            pl.BlockSpec((tm, d), lambda i, j: (i, 0)),
            _mod_spec(k0, tpb, d), _mod_spec(k0 + 1, tpb, d),
            pl.BlockSpec((d, tn), lambda i, j: (0, j)),
            pl.BlockSpec((1, tn), lambda i, j: (0, j)),
        ],
        out_specs=pl.BlockSpec((tm, tn), lambda i, j: (i, j)),
        out_shape=jax.ShapeDtypeStruct((t, n_out), BF16),
        scratch_shapes=[pltpu.VMEM((tm, d), BF16)],
        compiler_params=_params("parallel", "arbitrary"),
        name="qkv_proj",
    )(x, mod3, mod3, w, col_scale)


def _lru_in_kernel(x_ref, sh_ref, sc_ref, w_ref, og_ref, ox_ref, h_ref, *, n_gate_tiles):
    j = pl.program_id(1)

    @pl.when(j == 0)
    def _():
        h_ref[...] = _modulate(x_ref[...], sh_ref[...], sc_ref[...]).astype(BF16)

    @pl.when(j < n_gate_tiles)
    def _():
        og_ref[...] = _gelu_tanh(_dot(h_ref[...], w_ref[...])).astype(og_ref.dtype)

    @pl.when(j >= n_gate_tiles)
    def _():
        ox_ref[...] = _dot(h_ref[...], w_ref[...])


def _lru_in_proj(x, mod3, k0, w, *, seq, tm=512, tn=1280):
    t, d = x.shape
    d_rnn = w.shape[1] // 2
    tpb = seq // tm
    ng = d_rnn // tn
    return pl.pallas_call(
        functools.partial(_lru_in_kernel, n_gate_tiles=ng),
        grid=(t // tm, 2 * ng),
        in_specs=[
            pl.BlockSpec((tm, d), lambda i, j: (i, 0)),
            _mod_spec(k0, tpb, d), _mod_spec(k0 + 1, tpb, d),
            pl.BlockSpec((d, tn), lambda i, j: (0, j)),
        ],
        out_specs=[pl.BlockSpec((tm, tn), lambda i, j: (i, jnp.minimum(j, ng - 1))),
                   pl.BlockSpec((tm, tn), lambda i, j: (i, jnp.maximum(j - ng, 0)))],
        out_shape=[jax.ShapeDtypeStruct((t, d_rnn), BF16), jax.ShapeDtypeStruct((t, d_rnn), F32)],
        scratch_shapes=[pltpu.VMEM((tm, d), BF16)],
        compiler_params=_params("parallel", "arbitrary"),
        name="lru_in_proj",
    )(x, mod3, mod3, w)


def _outproj_kernel(a_ref, w_ref, x_ref, gt_ref, lng_ref, lnb_ref, o_ref, *, alpha):
    for r0 in range(0, a_ref.shape[0], OUTPROJ_ROW_BLOCK):
        rows = slice(r0, r0 + OUTPROJ_ROW_BLOCK)
        y = _dot(a_ref[rows, :], w_ref[...])
        o_ref[rows, :] = _residual_layer_norm(x_ref[rows, :], y, gt_ref[...], 1.0, alpha,
                                              lng_ref[...], lnb_ref[...])


def _outproj(a, w, x, mod3, k_gate, ln_g, ln_b, *, seq, alpha, tm=512):
    t, d = x.shape
    k = a.shape[1]
    tpb = seq // tm
    return pl.pallas_call(
        functools.partial(_outproj_kernel, alpha=alpha),
        grid=(t // tm,),
        in_specs=[
            pl.BlockSpec((tm, k), lambda i: (i, 0)),
            pl.BlockSpec((k, d), lambda i: (0, 0), pipeline_mode=pl.Buffered(1)),
            pl.BlockSpec((tm, d), lambda i: (i, 0)),
            _mod_spec(k_gate, tpb, d),
            _row_spec(d), _row_spec(d),
        ],
        out_specs=pl.BlockSpec((tm, d), lambda i: (i, 0)),
        out_shape=jax.ShapeDtypeStruct((t, d), F32),
        compiler_params=_params("parallel"),
        name="outproj",
    )(a, w, x, mod3, ln_g.reshape(1, d), ln_b.reshape(1, d))


def _attn_kernel(lq1_ref, lk1_ref, lq2_ref, lk2_ref, sg_ref, q_ref, k_ref, v_ref, o_ref,
                 s0_ref, s1_ref, rm0_ref, rm1_ref, p0_ref, p1_ref, c0_ref, c1_ref,
                 m_ref, l_ref, acc_ref,
                 e_ref, bias_ref, *, tq, lambda_init):
    dh = DIFF_HEAD_DIM
    lanes = l_ref.shape[-1]
    seq = q_ref.shape[0]
    nq = seq // tq
    n_pairs = nq * (nq + 1) // 2
    assert n_pairs % ATTN_STEPS_PER_TRIP == 0 and ATTN_STEPS_PER_TRIP % 2 == 0
    assert tq // CHUNK <= lanes
    nt = (((1,), (1,)), ((), ()))

    r = lax.broadcasted_iota(jnp.int32, (tq, lanes), 0) // CHUNK
    c = lax.broadcasted_iota(jnp.int32, (tq, lanes), 1)
    e_ref[...] = jnp.where(r == c, 1.0, 0.0).astype(BF16)
    bias_ref[0] = jnp.zeros((tq, lanes), BF16)
    bias_ref[1] = jnp.where((r <= c) | (c >= tq // CHUNK), 0.0, NEG_BIG).astype(BF16)
    m_ref[...] = jnp.full(m_ref.shape, NEG_BIG, F32)
    l_ref[...] = jnp.zeros(l_ref.shape, F32)
    acc_ref[...] = jnp.zeros(acc_ref.shape, F32)

    def scores(qi, kt, slot):
        s_out, rm_out = slot[0], slot[1]
        q0 = pl.multiple_of(qi * tq, tq)
        k0 = pl.multiple_of(kt * tq, tq)
        bias = bias_ref[jnp.where(kt == qi, 1, 0)]
        e = e_ref[...]
        for s in range(2):
            qa = jnp.concatenate([q_ref[pl.ds(q0, tq), s * dh:(s + 1) * dh], e], axis=1)
            ka = jnp.concatenate([k_ref[pl.ds(k0, tq), s * dh:(s + 1) * dh], bias], axis=1)
            sc = lax.dot_general(qa, ka, nt, preferred_element_type=F32)
            s_out[s] = sc
            mx = functools.reduce(jnp.maximum,
                                  [sc[:, j * lanes:(j + 1) * lanes] for j in range(tq // lanes)])
            rm_out[s] = jnp.broadcast_to(jnp.max(mx, axis=-1, keepdims=True), (tq, lanes))

    def fold(qi, kt, slot):
        s_in, rm_in, p_buf, c_buf = slot
        q0 = pl.multiple_of(qi * tq, tq)
        v = v_ref[pl.ds(pl.multiple_of(kt * tq, tq), tq), :]
        for s in range(2):
            for r0 in range(0, tq, ATTN_ROW_BLOCK):
                rows = slice(r0, r0 + ATTN_ROW_BLOCK)
                srows = pl.ds(pl.multiple_of(q0 + r0, ATTN_ROW_BLOCK), ATTN_ROW_BLOCK)
                sc = [s_in[s, rows, j * lanes:(j + 1) * lanes] for j in range(tq // lanes)]
                m_old = m_ref[s, srows, :]
                m_new = jnp.maximum(m_old, rm_in[s, rows, :])
                corr = jnp.exp2(m_old - m_new)
                p = [jnp.exp2(x - m_new) for x in sc]
                l_ref[s, srows, :] = corr * l_ref[s, srows, :] + functools.reduce(jnp.add, p)
                m_ref[s, srows, :] = m_new
                c_buf[s, rows, :] = corr
                p_buf[s, rows, :] = jnp.concatenate(p, axis=1).astype(BF16)
            pv = _dot(p_buf[s], v)
            corr = c_buf[s]
            qrows = pl.ds(q0, tq)
            acc_ref[s, qrows, :] = jnp.concatenate([corr, corr], axis=1) * acc_ref[s, qrows, :] + pv

    slots = ((s0_ref, rm0_ref, p0_ref, c0_ref), (s1_ref, rm1_ref, p1_ref, c1_ref))

    def step(qi, kt, cur, nxt):
        last = kt == qi
        nqi = jnp.where(last, qi + 1, qi)
        nkt = jnp.where(last, 0, kt + 1)
        scores(jnp.minimum(nqi, nq - 1), nkt, nxt)
        fold(qi, kt, cur)
        return nqi, nkt

    def trip(_, carry):
        qi, kt = carry
        for _ in range(ATTN_STEPS_PER_TRIP // 2):
            qi, kt = step(qi, kt, slots[0], slots[1])
            qi, kt = step(qi, kt, slots[1], slots[0])
        return qi, kt

    scores(0, 0, slots[0])
    lax.fori_loop(0, n_pairs // ATTN_STEPS_PER_TRIP, trip, (jnp.int32(0), jnp.int32(0)))

    lam = (jnp.exp(jnp.sum(lq1_ref[...] * lk1_ref[...], axis=-1, keepdims=True))
           - jnp.exp(jnp.sum(lq2_ref[...] * lk2_ref[...], axis=-1, keepdims=True))
           + lambda_init)
    out_scale = sg_ref[...] * (1.0 - lambda_init)

    def normalise(i, carry):
        rows = pl.ds(pl.multiple_of(i * ATTN_ROW_BLOCK, ATTN_ROW_BLOCK), ATTN_ROW_BLOCK)
        inv1 = 1.0 / jnp.sum(l_ref[0, rows, :], axis=-1, keepdims=True)
        inv2 = lam / jnp.sum(l_ref[1, rows, :], axis=-1, keepdims=True)
        o = acc_ref[0, rows, :] * inv1 - acc_ref[1, rows, :] * inv2
        o = o * lax.rsqrt(jnp.mean(o * o, axis=-1, keepdims=True) + RMS_EPS)
        o_ref[rows, :] = (o * out_scale).astype(o_ref.dtype)
        return carry

    lax.fori_loop(0, seq // ATTN_ROW_BLOCK, normalise, 0, unroll=4)


def _diff_attention(qkv, lq1, lk1, lq2, lk2, subln_g, *, batch, seq, heads, lambda_init, tq=512):
    dv = 2 * DIFF_HEAD_DIM
    lanes = 128
    vec = pl.BlockSpec((1, DIFF_HEAD_DIM), lambda b, h: (0, 0))
    return pl.pallas_call(
        functools.partial(_attn_kernel, tq=tq, lambda_init=lambda_init),
        grid=(batch, heads),
        in_specs=[
            vec, vec, vec, vec,
            pl.BlockSpec((1, dv), lambda b, h: (0, 0)),
            pl.BlockSpec((seq, dv), lambda b, h: (b, h)),
            pl.BlockSpec((seq, dv), lambda b, h: (b, heads + h)),
            pl.BlockSpec((seq, dv), lambda b, h: (b, 2 * heads + h)),
        ],
        out_specs=pl.BlockSpec((seq, dv), lambda b, h: (b, h)),
        out_shape=jax.ShapeDtypeStruct((batch * seq, heads * dv), BF16),
        scratch_shapes=[
            pltpu.VMEM((2, tq, tq), F32), pltpu.VMEM((2, tq, tq), F32),
            pltpu.VMEM((2, tq, lanes), F32), pltpu.VMEM((2, tq, lanes), F32),
            pltpu.VMEM((2, tq, tq), BF16), pltpu.VMEM((2, tq, tq), BF16),
            pltpu.VMEM((2, tq, lanes), F32), pltpu.VMEM((2, tq, lanes), F32),
            pltpu.VMEM((2, seq, lanes), F32), pltpu.VMEM((2, seq, lanes), F32),
            pltpu.VMEM((2, seq, dv), F32),
            pltpu.VMEM((tq, lanes), BF16), pltpu.VMEM((2, tq, lanes), BF16),
        ],
        compiler_params=_params("parallel", "parallel"),
        name="diff_attn",
    )(lq1.reshape(1, -1), lk1.reshape(1, -1), lq2.reshape(1, -1), lk2.reshape(1, -1),
      subln_g.reshape(1, dv), qkv, qkv, qkv)


def _lru_kernel(x_ref, g_ref, cw_ref, cb_ref, gaw_ref, gab_ref, gxw_ref, gxb_ref, lam_ref,
                o_ref, xe_ref, hc_ref, a_s, b_s, h_s, *, tt):
    t = pl.program_id(2)
    w = x_ref.shape[1]

    @pl.when(t == 0)
    def _():
        xe_ref[...] = jnp.zeros_like(xe_ref)
        hc_ref[...] = jnp.zeros_like(hc_ref)

    ng = tt // 8
    x3 = x_ref[...].reshape(ng, 8, w)
    xall = jnp.concatenate([xe_ref[...][None], x3], axis=0)
    xe_ref[...] = x3[ng - 1]
    row = lax.broadcasted_iota(jnp.int32, (ng, 8, w), 1)
    cw = cw_ref[...]
    xc = cw[CONV_WIDTH - 1:CONV_WIDTH, :] * x3 + cb_ref[...]
    for s in range(1, CONV_WIDTH):
        rot = pltpu.roll(xall, s, 1)
        shifted = jnp.where(row < s, rot[:ng], rot[1:])
        xc = xc + cw[CONV_WIDTH - 1 - s:CONV_WIDTH - s, :] * shifted

    xcb = xc.reshape(tt, w).astype(BF16)
    r = jax.nn.sigmoid(_dot(xcb, gaw_ref[...]) + gab_ref[...]).reshape(ng, 8, w)
    gi = jax.nn.sigmoid(_dot(xcb, gxw_ref[...]) + gxb_ref[...]).reshape(ng, 8, w)
    z = -lam_ref[...]
    softplus = jnp.maximum(z, 0.0) + jnp.log(1.0 + jnp.exp(-jnp.abs(z)))
    a = jnp.exp2(r * ((-LRU_C * math.log2(math.e)) * softplus))
    mult = jnp.exp2(0.5 * jnp.log2(jnp.maximum(1.0 - a * a, 0.0)))
    bx = xc * gi * mult

    for d in (1, 2, 4):
        keep = row >= d
        a_sh = jnp.where(keep, pltpu.roll(a, d, 1), 1.0)
        b_sh = jnp.where(keep, pltpu.roll(bx, d, 1), 0.0)
        bx = a * b_sh + bx
        a = a * a_sh
    a_s[...] = a.reshape(tt, w)
    b_s[...] = bx.reshape(tt, w)

    def body(g, carry):
        off = pl.multiple_of(g * 8, 8)
        h = a_s[pl.ds(off, 8), :] * carry + b_s[pl.ds(off, 8), :]
        h_s[pl.ds(off, 8), :] = h
        return h[7:8, :]

    carry = lax.fori_loop(0, tt // 8, body, hc_ref[0:1, :], unroll=8)
    hc_ref[0:1, :] = carry
    o_ref[...] = (h_s[...] * g_ref[...].astype(F32)).astype(o_ref.dtype)


def _lru_core(xb, gb, conv_w, conv_b, ga_w, ga_b, gx_w, gx_b, lam, *, batch, seq, tt=512):
    t, d_rnn = xb.shape
    bw = LRU_BLOCK_W
    nb = d_rnn // bw
    nt = seq // tt
    tile = lambda b, n, s: (b * nt + s, n)
    row = lambda b, n, s: (0, n)
    gate_w = lambda b, n, s: (n, 0, 0)
    return pl.pallas_call(
        functools.partial(_lru_kernel, tt=tt),
        grid=(batch, nb, nt),
        in_specs=[
            pl.BlockSpec((tt, bw), tile),
            pl.BlockSpec((tt, bw), tile),
            pl.BlockSpec((CONV_WIDTH, bw), row),
            pl.BlockSpec((1, bw), row),
            pl.BlockSpec((None, bw, bw), gate_w),
            pl.BlockSpec((1, bw), row),
            pl.BlockSpec((None, bw, bw), gate_w),
            pl.BlockSpec((1, bw), row),
            pl.BlockSpec((1, bw), row),
        ],
        out_specs=pl.BlockSpec((tt, bw), tile),
        out_shape=jax.ShapeDtypeStruct((t, d_rnn), BF16),
        scratch_shapes=[
            pltpu.VMEM((8, bw), F32),
            pltpu.VMEM((8, bw), F32),
            pltpu.VMEM((tt, bw), F32),
            pltpu.VMEM((tt, bw), F32),
            pltpu.VMEM((tt, bw), F32),
        ],
        compiler_params=_params("parallel", "parallel", "arbitrary"),
        name="lru_core",
    )(xb, gb, conv_w, conv_b.reshape(1, -1), ga_w, ga_b.reshape(1, -1), gx_w, gx_b.reshape(1, -1),
      lam.reshape(1, -1))


def kernel(x, c, ada_w, ada_b, ln_g, ln_b, ffn_w_in, ffn_w_out, attn_w_qkv, attn_w_o, attn_lambda_q1, attn_lambda_k1, attn_lambda_q2, attn_lambda_k2, attn_subln_g, lru_w_in, lru_conv_w, lru_conv_b, lru_gate_a_w, lru_gate_a_b, lru_gate_x_w, lru_gate_x_b, lru_lambda, lru_w_out):
    batch, seq, d = x.shape
    depth = ada_w.shape[0]
    d_rnn = lru_w_out.shape[1]
    heads = d // (2 * DIFF_HEAD_DIM)
    alpha = (2 * depth) ** 0.25
    n_mixers = 2

    mod = _adaln(c, ada_w, ada_b)
    xt = x.reshape(batch * seq, d)
    q_scale = jnp.concatenate([jnp.full((1, d), DIFF_HEAD_DIM ** -0.5 * math.log2(math.e), F32),
                               jnp.ones((1, 2 * d), F32)], axis=1)
    ffn_in = ffn_w_in.astype(BF16).reshape((depth * 2,) + ffn_w_in.shape[2:])
    ffn_out = ffn_w_out.astype(BF16).reshape((depth * 2,) + ffn_w_out.shape[2:])

    for i in range(depth):
        mod3 = mod[i].reshape(batch * 9, 1, d)
        xt = _ffn(xt, mod3, 0, ffn_in, ffn_out, 2 * i, ln_g[i, 0], ln_b[i, 0], seq=seq, alpha=alpha)
        j = i // n_mixers
        if i % n_mixers == 0:
            lambda_init = 0.8 - 0.6 * math.exp(-0.3 * i)
            qkv = _qkv_proj(xt, mod3, 3, attn_w_qkv[j].astype(BF16), q_scale, seq=seq)
            mixed = _diff_attention(qkv, attn_lambda_q1[j], attn_lambda_k1[j], attn_lambda_q2[j],
                                    attn_lambda_k2[j], attn_subln_g[j], batch=batch, seq=seq,
                                    heads=heads, lambda_init=lambda_init)
            w_o = attn_w_o[j].astype(BF16)
        else:
            gb, xb = _lru_in_proj(xt, mod3, 3, lru_w_in[j].astype(BF16), seq=seq)
            mixed = _lru_core(xb, gb, lru_conv_w[j], lru_conv_b[j], lru_gate_a_w[j].astype(BF16),
                              lru_gate_a_b[j], lru_gate_x_w[j].astype(BF16), lru_gate_x_b[j],
                              lru_lambda[j], batch=batch, seq=seq)
            w_o = lru_w_out[j].astype(BF16)
        xt = _outproj(mixed, w_o, xt, mod3, 5, ln_g[i, 1], ln_b[i, 1], seq=seq, alpha=alpha)
        xt = _ffn(xt, mod3, 6, ffn_in, ffn_out, 2 * i + 1, ln_g[i, 2], ln_b[i, 2], seq=seq,
                  alpha=alpha)
    return xt.reshape(batch, seq, d)
```

```python
import functools
import math

import jax
import jax.numpy as jnp
from jax import lax
from jax.experimental import pallas as pl
from jax.experimental.pallas import tpu as pltpu

F32 = jnp.float32
BF16 = jnp.bfloat16

CHUNK = 64
DIFF_HEAD_DIM = 128
CONV_WIDTH = 4
LRU_BLOCK_W = 256
LRU_C = 8.0
LN_EPS = 1e-5
RMS_EPS = 1e-5
NEG_BIG = -1e30
LN_ROW_BLOCK = 64
OUTPROJ_ROW_BLOCK = 256
ATTN_ROW_BLOCK = 64
ATTN_PV_ROWS = 512
ATTN_STEPS_PER_TRIP = 4

VMEM_LIMIT_BYTES = 56 * 1024 * 1024


def _params(*sem):
    return pltpu.CompilerParams(dimension_semantics=sem, vmem_limit_bytes=VMEM_LIMIT_BYTES)


def _dot(a, b):
    return jnp.dot(a, b, preferred_element_type=F32)


def _modulate(x, shift_row, scale_row):
    return x * (1.0 + scale_row) + shift_row


def _residual_layer_norm(x, y, gate_row, res_w, alpha, ln_g, ln_b):
    z = alpha * x + (res_w * (1.0 + gate_row)) * y
    mu = jnp.mean(z, axis=-1, keepdims=True)
    zc = z - mu
    var = jnp.mean(zc * zc, axis=-1, keepdims=True)
    return zc * lax.rsqrt(var + LN_EPS) * ln_g + ln_b


def _adaln_kernel(c_ref, w_ref, b_ref, o_ref):
    c = c_ref[...]
    c_act = (c * jax.nn.sigmoid(c)).astype(BF16)
    o_ref[...] = _dot(c_act, w_ref[...].astype(BF16)) + b_ref[...]


def _adaln(c, ada_w, ada_b, *, tn=1024):
    depth, d, n = ada_w.shape
    b = c.shape[0]
    return pl.pallas_call(
        _adaln_kernel,
        grid=(depth, n // tn),
        in_specs=[
            pl.BlockSpec((b, d), lambda l, j: (0, 0)),
            pl.BlockSpec((None, d, tn), lambda l, j: (l, 0, j)),
            pl.BlockSpec((None, 1, tn), lambda l, j: (l, 0, j)),
        ],
        out_specs=pl.BlockSpec((None, b, tn), lambda l, j: (l, 0, j)),
        out_shape=jax.ShapeDtypeStruct((depth, b, n), F32),
        compiler_params=_params("parallel", "parallel"),
        name="adaln",
    )(c, ada_w, ada_b.reshape(depth, 1, n))


def _mod_spec(k, tiles_per_batch, d):
    return pl.BlockSpec((None, 1, d), lambda i, *_: ((i // tiles_per_batch) * 9 + k, 0, 0))


def _row_spec(d):
    return pl.BlockSpec((1, d), lambda *_: (0, 0))


def _ffn_kernel(x_ref, sh_ref, sc_ref, gt_ref, wa_ref, wu_ref, wo_ref, lng_ref, lnb_ref,
                o_ref, h_ref, acc_ref, *, alpha):
    j = pl.program_id(1)

    @pl.when(j == 0)
    def _():
        h_ref[...] = _modulate(x_ref[...], sh_ref[...], sc_ref[...]).astype(BF16)
        acc_ref[...] = jnp.zeros_like(acc_ref)

    h = h_ref[...]
    a = _dot(h, wa_ref[...])
    u = _dot(h, wu_ref[...])
    g = (a * jax.nn.sigmoid(a) * u).astype(BF16)
    acc_ref[...] += _dot(g, wo_ref[...])

    @pl.when(j == pl.num_programs(1) - 1)
    def _():
        def norm_rows(i, carry):
            rows = pl.ds(pl.multiple_of(i * LN_ROW_BLOCK, LN_ROW_BLOCK), LN_ROW_BLOCK)
            o_ref[rows, :] = _residual_layer_norm(x_ref[rows, :], acc_ref[rows, :], gt_ref[...],
                                                  0.5, alpha, lng_ref[...], lnb_ref[...])
            return carry

        lax.fori_loop(0, x_ref.shape[0] // LN_ROW_BLOCK, norm_rows, 0, unroll=2)


def _ffn(x, mod3, k0, w_in, w_out, widx, ln_g, ln_b, *, seq, alpha, tm=1024, tf=512):
    t, d = x.shape
    d_ff = w_out.shape[1]
    nf = d_ff // tf
    tpb = seq // tm
    return pl.pallas_call(
        functools.partial(_ffn_kernel, alpha=alpha),
        grid=(t // tm, nf),
        in_specs=[
            pl.BlockSpec((tm, d), lambda i, j: (i, 0)),
            _mod_spec(k0, tpb, d), _mod_spec(k0 + 1, tpb, d), _mod_spec(k0 + 2, tpb, d),
            pl.BlockSpec((None, d, tf), lambda i, j: (widx, 0, j)),
            pl.BlockSpec((None, d, tf), lambda i, j: (widx, 0, j + nf)),
            pl.BlockSpec((None, tf, d), lambda i, j: (widx, j, 0)),
            _row_spec(d), _row_spec(d),
        ],
        out_specs=pl.BlockSpec((tm, d), lambda i, j: (i, 0), pipeline_mode=pl.Buffered(1)),
        out_shape=jax.ShapeDtypeStruct((t, d), F32),
        scratch_shapes=[pltpu.VMEM((tm, d), BF16), pltpu.VMEM((tm, d), F32)],
        compiler_params=_params("parallel", "arbitrary"),
        name="ffn",
    )(x, mod3, mod3, mod3, w_in, w_in, w_out, ln_g.reshape(1, d), ln_b.reshape(1, d))


def _gelu_tanh(x):
    return 0.5 * x * (1.0 + jnp.tanh(math.sqrt(2.0 / math.pi) * (x + 0.044715 * (x * x * x))))


def _qkv_kernel(x_ref, sh_ref, sc_ref, w_ref, cs_ref, o_ref, h_ref):
    @pl.when(pl.program_id(1) == 0)
    def _():
        h_ref[...] = _modulate(x_ref[...], sh_ref[...], sc_ref[...]).astype(BF16)

    o_ref[...] = (_dot(h_ref[...], w_ref[...]) * cs_ref[...]).astype(o_ref.dtype)


def _qkv_proj(x, mod3, k0, w, col_scale, *, seq, tm=1024, tn=1536):
    t, d = x.shape
    n_out = w.shape[1]
    tpb = seq // tm
    return pl.pallas_call(
        _qkv_kernel,
        grid=(t // tm, n_out // tn),
        in_specs=[
            pl.BlockSpec((tm, d), lambda i, j: (i, 0)),
            _mod_spec(k0, tpb, d), _mod_spec(k0 + 1, tpb, d),
            pl.BlockSpec((d, tn), lambda i, j: (0, j)),
            pl.BlockSpec((1, tn), lambda i, j: (0, j)),
        ],
        out_specs=pl.BlockSpec((tm, tn), lambda i, j: (i, j)),
        out_shape=jax.ShapeDtypeStruct((t, n_out), BF16),
        scratch_shapes=[pltpu.VMEM((tm, d), BF16)],
        compiler_params=_params("parallel", "arbitrary"),
        name="qkv_proj",
    )(x, mod3, mod3, w, col_scale)


def _lru_in_kernel(x_ref, sh_ref, sc_ref, w_ref, og_ref, ox_ref, h_ref, *, n_gate_tiles):
    j = pl.program_id(1)

    @pl.when(j == 0)
    def _():
        h_ref[...] = _modulate(x_ref[...], sh_ref[...], sc_ref[...]).astype(BF16)

    @pl.when(j < n_gate_tiles)
    def _():
        og_ref[...] = _gelu_tanh(_dot(h_ref[...], w_ref[...])).astype(og_ref.dtype)

    @pl.when(j >= n_gate_tiles)
    def _():
        ox_ref[...] = _dot(h_ref[...], w_ref[...])


def _lru_in_proj(x, mod3, k0, w, *, seq, tm=1024, tn=1280):
    t, d = x.shape
    d_rnn = w.shape[1] // 2
    tpb = seq // tm
    ng = d_rnn // tn
    return pl.pallas_call(
        functools.partial(_lru_in_kernel, n_gate_tiles=ng),
        grid=(t // tm, 2 * ng),
        in_specs=[
            pl.BlockSpec((tm, d), lambda i, j: (i, 0)),
            _mod_spec(k0, tpb, d), _mod_spec(k0 + 1, tpb, d),
            pl.BlockSpec((d, tn), lambda i, j: (0, j)),
        ],
        out_specs=[pl.BlockSpec((tm, tn), lambda i, j: (i, jnp.minimum(j, ng - 1))),
                   pl.BlockSpec((tm, tn), lambda i, j: (i, jnp.maximum(j - ng, 0)))],
        out_shape=[jax.ShapeDtypeStruct((t, d_rnn), BF16), jax.ShapeDtypeStruct((t, d_rnn), F32)],
        scratch_shapes=[pltpu.VMEM((tm, d), BF16)],
        compiler_params=_params("parallel", "arbitrary"),
        name="lru_in_proj",
    )(x, mod3, mod3, w)


def _outproj_kernel(a_ref, w_ref, x_ref, gt_ref, lng_ref, lnb_ref, o_ref, *, alpha):
    for r0 in range(0, a_ref.shape[0], OUTPROJ_ROW_BLOCK):
        rows = slice(r0, r0 + OUTPROJ_ROW_BLOCK)
        y = _dot(a_ref[rows, :], w_ref[...])
        o_ref[rows, :] = _residual_layer_norm(x_ref[rows, :], y, gt_ref[...], 1.0, alpha,
                                              lng_ref[...], lnb_ref[...])


def _outproj(a, w, x, mod3, k_gate, ln_g, ln_b, *, seq, alpha, tm=512):
    t, d = x.shape
    k = a.shape[1]
    tpb = seq // tm
    return pl.pallas_call(
        functools.partial(_outproj_kernel, alpha=alpha),
        grid=(t // tm,),
        in_specs=[
            pl.BlockSpec((tm, k), lambda i: (i, 0)),
            pl.BlockSpec((k, d), lambda i: (0, 0), pipeline_mode=pl.Buffered(1)),
            pl.BlockSpec((tm, d), lambda i: (i, 0)),
            _mod_spec(k_gate, tpb, d),
            _row_spec(d), _row_spec(d),
        ],
        out_specs=pl.BlockSpec((tm, d), lambda i: (i, 0)),
        out_shape=jax.ShapeDtypeStruct((t, d), F32),
        compiler_params=_params("parallel"),
        name="outproj",
    )(a, w, x, mod3, ln_g.reshape(1, d), ln_b.reshape(1, d))


def _attn_kernel(lq1_ref, lk1_ref, lq2_ref, lk2_ref, sg_ref, q_ref, k_ref, v_ref, o_ref,
                 s0_ref, s1_ref, rm0_ref, rm1_ref, p0_ref, p1_ref, c0_ref, c1_ref,
                 m_ref, l_ref, acc_ref,
                 e_ref, bias_ref, *, tq, lambda_init):
    dh = DIFF_HEAD_DIM
    lanes = l_ref.shape[-1]
    seq = q_ref.shape[0]
    nq = seq // tq
    n_pairs = nq * (nq + 1) // 2
    assert n_pairs % ATTN_STEPS_PER_TRIP == 0 and ATTN_STEPS_PER_TRIP % 2 == 0
    assert tq // CHUNK <= lanes
    nt = (((1,), (1,)), ((), ()))

    r = lax.broadcasted_iota(jnp.int32, (tq, lanes), 0) // CHUNK
    c = lax.broadcasted_iota(jnp.int32, (tq, lanes), 1)
    e_ref[...] = jnp.where(r == c, 1.0, 0.0).astype(BF16)
    bias_ref[0] = jnp.zeros((tq, lanes), BF16)
    bias_ref[1] = jnp.where((r <= c) | (c >= tq // CHUNK), 0.0, NEG_BIG).astype(BF16)
    m_ref[...] = jnp.full(m_ref.shape, NEG_BIG, F32)
    l_ref[...] = jnp.zeros(l_ref.shape, F32)
    acc_ref[...] = jnp.zeros(acc_ref.shape, F32)

    def scores(qi, kt, slot):
        s_out, rm_out = slot[0], slot[1]
        q0 = pl.multiple_of(qi * tq, tq)
        k0 = pl.multiple_of(kt * tq, tq)
        bias = bias_ref[jnp.where(kt == qi, 1, 0)]
        e = e_ref[...]
        for s in range(2):
            qa = jnp.concatenate([q_ref[pl.ds(q0, tq), s * dh:(s + 1) * dh], e], axis=1)
            ka = jnp.concatenate([k_ref[pl.ds(k0, tq), s * dh:(s + 1) * dh], bias], axis=1)
            sc = lax.dot_general(qa, ka, nt, preferred_element_type=F32)
            s_out[s] = sc
            mx = functools.reduce(jnp.maximum,
                                  [sc[:, j * lanes:(j + 1) * lanes] for j in range(tq // lanes)])
            rm_out[s] = jnp.broadcast_to(jnp.max(mx, axis=-1, keepdims=True), (tq, lanes))

    def fold(qi, kt, slot):
        s_in, rm_in, p_buf, c_buf = slot
        q0 = pl.multiple_of(qi * tq, tq)
        v = v_ref[pl.ds(pl.multiple_of(kt * tq, tq), tq), :]
        for s, h0 in [(s, h0) for s in range(2) for h0 in range(0, tq, ATTN_PV_ROWS)]:
            for r0 in range(h0, h0 + ATTN_PV_ROWS, ATTN_ROW_BLOCK):
                rows = slice(r0, r0 + ATTN_ROW_BLOCK)
                srows = pl.ds(pl.multiple_of(q0 + r0, ATTN_ROW_BLOCK), ATTN_ROW_BLOCK)
                sc = [s_in[s, rows, j * lanes:(j + 1) * lanes] for j in range(tq // lanes)]
                m_old = m_ref[s, srows, :]
                m_new = jnp.maximum(m_old, rm_in[s, rows, :])
                corr = jnp.exp2(m_old - m_new)
                p = [jnp.exp2(x - m_new) for x in sc]
                l_ref[s, srows, :] = corr * l_ref[s, srows, :] + functools.reduce(jnp.add, p)
                m_ref[s, srows, :] = m_new
                c_buf[s, rows, :] = corr
                p_buf[s, rows, :] = jnp.concatenate(p, axis=1).astype(BF16)
            hrows = slice(h0, h0 + ATTN_PV_ROWS)
            pv = _dot(p_buf[s, hrows, :], v)
            corr = c_buf[s, hrows, :]
            qrows = pl.ds(pl.multiple_of(q0 + h0, ATTN_PV_ROWS), ATTN_PV_ROWS)
            acc_ref[s, qrows, :] = jnp.concatenate([corr, corr], axis=1) * acc_ref[s, qrows, :] + pv

    slots = ((s0_ref, rm0_ref, p0_ref, c0_ref), (s1_ref, rm1_ref, p1_ref, c1_ref))

    def step(qi, kt, cur, nxt):
        last = kt == qi
        nqi = jnp.where(last, qi + 1, qi)
        nkt = jnp.where(last, 0, kt + 1)
        scores(jnp.minimum(nqi, nq - 1), nkt, nxt)
        fold(qi, kt, cur)
        return nqi, nkt

    def trip(_, carry):
        qi, kt = carry
        for _ in range(ATTN_STEPS_PER_TRIP // 2):
            qi, kt = step(qi, kt, slots[0], slots[1])
            qi, kt = step(qi, kt, slots[1], slots[0])
        return qi, kt

    scores(0, 0, slots[0])
    lax.fori_loop(0, n_pairs // ATTN_STEPS_PER_TRIP, trip, (jnp.int32(0), jnp.int32(0)))

    lam = (jnp.exp(jnp.sum(lq1_ref[...] * lk1_ref[...], axis=-1, keepdims=True))
           - jnp.exp(jnp.sum(lq2_ref[...] * lk2_ref[...], axis=-1, keepdims=True))
           + lambda_init)
    out_scale = sg_ref[...] * (1.0 - lambda_init)

    def normalise(i, carry):
        rows = pl.ds(pl.multiple_of(i * ATTN_ROW_BLOCK, ATTN_ROW_BLOCK), ATTN_ROW_BLOCK)
        inv1 = 1.0 / jnp.sum(l_ref[0, rows, :], axis=-1, keepdims=True)
        inv2 = lam / jnp.sum(l_ref[1, rows, :], axis=-1, keepdims=True)
        o = acc_ref[0, rows, :] * inv1 - acc_ref[1, rows, :] * inv2
        o = o * lax.rsqrt(jnp.mean(o * o, axis=-1, keepdims=True) + RMS_EPS)
        o_ref[rows, :] = (o * out_scale).astype(o_ref.dtype)
        return carry

    lax.fori_loop(0, seq // ATTN_ROW_BLOCK, normalise, 0, unroll=4)


def _diff_attention(qkv, lq1, lk1, lq2, lk2, subln_g, *, batch, seq, heads, lambda_init, tq=512):
    dv = 2 * DIFF_HEAD_DIM
    lanes = 128
    vec = pl.BlockSpec((1, DIFF_HEAD_DIM), lambda b, h: (0, 0))
    return pl.pallas_call(
        functools.partial(_attn_kernel, tq=tq, lambda_init=lambda_init),
        grid=(batch, heads),
        in_specs=[
            vec, vec, vec, vec,
            pl.BlockSpec((1, dv), lambda b, h: (0, 0)),
            pl.BlockSpec((seq, dv), lambda b, h: (b, h)),
            pl.BlockSpec((seq, dv), lambda b, h: (b, heads + h)),
            pl.BlockSpec((seq, dv), lambda b, h: (b, 2 * heads + h)),
        ],
        out_specs=pl.BlockSpec((seq, dv), lambda b, h: (b, h)),
        out_shape=jax.ShapeDtypeStruct((batch * seq, heads * dv), BF16),
        scratch_shapes=[
            pltpu.VMEM((2, tq, tq), F32), pltpu.VMEM((2, tq, tq), F32),
            pltpu.VMEM((2, tq, lanes), F32), pltpu.VMEM((2, tq, lanes), F32),
            pltpu.VMEM((2, tq, tq), BF16), pltpu.VMEM((2, tq, tq), BF16),
            pltpu.VMEM((2, tq, lanes), F32), pltpu.VMEM((2, tq, lanes), F32),
            pltpu.VMEM((2, seq, lanes), F32), pltpu.VMEM((2, seq, lanes), F32),
            pltpu.VMEM((2, seq, dv), F32),
            pltpu.VMEM((tq, lanes), BF16), pltpu.VMEM((2, tq, lanes), BF16),
        ],
        compiler_params=_params("parallel", "parallel"),
        name="diff_attn",
    )(lq1.reshape(1, -1), lk1.reshape(1, -1), lq2.reshape(1, -1), lk2.reshape(1, -1),
      subln_g.reshape(1, dv), qkv, qkv, qkv)


def _lru_kernel(x_ref, g_ref, cw_ref, cb_ref, gaw_ref, gab_ref, gxw_ref, gxb_ref, lam_ref,
                o_ref, xe_ref, hc_ref, a_s, b_s, h_s, *, tt):
    t = pl.program_id(2)
    w = x_ref.shape[1]

    @pl.when(t == 0)
    def _():
        xe_ref[...] = jnp.zeros_like(xe_ref)
        hc_ref[...] = jnp.zeros_like(hc_ref)

    ng = tt // 8
    x3 = x_ref[...].reshape(ng, 8, w)
    xall = jnp.concatenate([xe_ref[...][None], x3], axis=0)
    xe_ref[...] = x3[ng - 1]
    row = lax.broadcasted_iota(jnp.int32, (ng, 8, w), 1)
    cw = cw_ref[...]
    xc = cw[CONV_WIDTH - 1:CONV_WIDTH, :] * x3 + cb_ref[...]
    for s in range(1, CONV_WIDTH):
        rot = pltpu.roll(xall, s, 1)
        shifted = jnp.where(row < s, rot[:ng], rot[1:])
        xc = xc + cw[CONV_WIDTH - 1 - s:CONV_WIDTH - s, :] * shifted

    xcb = xc.reshape(tt, w).astype(BF16)
    r = jax.nn.sigmoid(_dot(xcb, gaw_ref[...]) + gab_ref[...]).reshape(ng, 8, w)
    gi = jax.nn.sigmoid(_dot(xcb, gxw_ref[...]) + gxb_ref[...]).reshape(ng, 8, w)
    z = -lam_ref[...]
    softplus = jnp.maximum(z, 0.0) + jnp.log(1.0 + jnp.exp(-jnp.abs(z)))
    a = jnp.exp2(r * ((-LRU_C * math.log2(math.e)) * softplus))
    mult = jnp.exp2(0.5 * jnp.log2(jnp.maximum(1.0 - a * a, 0.0)))
    bx = xc * gi * mult

    for d in (1, 2, 4):
        keep = row >= d
        a_sh = jnp.where(keep, pltpu.roll(a, d, 1), 1.0)
        b_sh = jnp.where(keep, pltpu.roll(bx, d, 1), 0.0)
        bx = a * b_sh + bx
        a = a * a_sh
    a_s[...] = a.reshape(tt, w)
    b_s[...] = bx.reshape(tt, w)

    def body(g, carry):
        off = pl.multiple_of(g * 8, 8)
        h = a_s[pl.ds(off, 8), :] * carry + b_s[pl.ds(off, 8), :]
        h_s[pl.ds(off, 8), :] = h
        return h[7:8, :]

    carry = lax.fori_loop(0, tt // 8, body, hc_ref[0:1, :], unroll=8)
    hc_ref[0:1, :] = carry
    o_ref[...] = (h_s[...] * g_ref[...].astype(F32)).astype(o_ref.dtype)


def _lru_core(xb, gb, conv_w, conv_b, ga_w, ga_b, gx_w, gx_b, lam, *, batch, seq, tt=1024):
    t, d_rnn = xb.shape
    bw = LRU_BLOCK_W
    nb = d_rnn // bw
    nt = seq // tt
    tile = lambda b, n, s: (b * nt + s, n)
    row = lambda b, n, s: (0, n)
    gate_w = lambda b, n, s: (n, 0, 0)
    return pl.pallas_call(
        functools.partial(_lru_kernel, tt=tt),
        grid=(batch, nb, nt),
        in_specs=[
            pl.BlockSpec((tt, bw), tile),
            pl.BlockSpec((tt, bw), tile),
            pl.BlockSpec((CONV_WIDTH, bw), row),
            pl.BlockSpec((1, bw), row),
            pl.BlockSpec((None, bw, bw), gate_w),
            pl.BlockSpec((1, bw), row),
            pl.BlockSpec((None, bw, bw), gate_w),
            pl.BlockSpec((1, bw), row),
            pl.BlockSpec((1, bw), row),
        ],
        out_specs=pl.BlockSpec((tt, bw), tile),
        out_shape=jax.ShapeDtypeStruct((t, d_rnn), BF16),
        scratch_shapes=[
            pltpu.VMEM((8, bw), F32),
            pltpu.VMEM((8, bw), F32),
            pltpu.VMEM((tt, bw), F32),
            pltpu.VMEM((tt, bw), F32),
            pltpu.VMEM((tt, bw), F32),
        ],
        compiler_params=_params("parallel", "parallel", "arbitrary"),
        name="lru_core",
    )(xb, gb, conv_w, conv_b.reshape(1, -1), ga_w, ga_b.reshape(1, -1), gx_w, gx_b.reshape(1, -1),
      lam.reshape(1, -1))


def kernel(x, c, ada_w, ada_b, ln_g, ln_b, ffn_w_in, ffn_w_out, attn_w_qkv, attn_w_o, attn_lambda_q1, attn_lambda_k1, attn_lambda_q2, attn_lambda_k2, attn_subln_g, lru_w_in, lru_conv_w, lru_conv_b, lru_gate_a_w, lru_gate_a_b, lru_gate_x_w, lru_gate_x_b, lru_lambda, lru_w_out):
    batch, seq, d = x.shape
    depth = ada_w.shape[0]
    d_rnn = lru_w_out.shape[1]
    heads = d // (2 * DIFF_HEAD_DIM)
    alpha = (2 * depth) ** 0.25
    n_mixers = 2

    mod = _adaln(c, ada_w, ada_b)
    xt = x.reshape(batch * seq, d)
    q_scale = jnp.concatenate([jnp.full((1, d), DIFF_HEAD_DIM ** -0.5 * math.log2(math.e), F32),
                               jnp.ones((1, 2 * d), F32)], axis=1)
    ffn_in = ffn_w_in.astype(BF16).reshape((depth * 2,) + ffn_w_in.shape[2:])
    ffn_out = ffn_w_out.astype(BF16).reshape((depth * 2,) + ffn_w_out.shape[2:])

    for i in range(depth):
        mod3 = mod[i].reshape(batch * 9, 1, d)
        xt = _ffn(xt, mod3, 0, ffn_in, ffn_out, 2 * i, ln_g[i, 0], ln_b[i, 0], seq=seq, alpha=alpha)
        j = i // n_mixers
        if i % n_mixers == 0:
            lambda_init = 0.8 - 0.6 * math.exp(-0.3 * i)
            qkv = _qkv_proj(xt, mod3, 3, attn_w_qkv[j].astype(BF16), q_scale, seq=seq)
            mixed = _diff_attention(qkv, attn_lambda_q1[j], attn_lambda_k1[j], attn_lambda_q2[j],
                                    attn_lambda_k2[j], attn_subln_g[j], batch=batch, seq=seq,
                                    heads=heads, lambda_init=lambda_init)
            w_o = attn_w_o[j].astype(BF16)
        else:
            gb, xb = _lru_in_proj(xt, mod3, 3, lru_w_in[j].astype(BF16), seq=seq)
            mixed = _lru_core(xb, gb, lru_conv_w[j], lru_conv_b[j], lru_gate_a_w[j].astype(BF16),
                              lru_gate_a_b[j], lru_gate_x_w[j].astype(BF16), lru_gate_x_b[j],
                              lru_lambda[j], batch=batch, seq=seq)
            w_o = lru_w_out[j].astype(BF16)
        xt = _outproj(mixed, w_o, xt, mod3, 5, ln_g[i, 1], ln_b[i, 1], seq=seq, alpha=alpha)
        xt = _ffn(xt, mod3, 6, ffn_in, ffn_out, 2 * i + 1, ln_g[i, 2], ln_b[i, 2], seq=seq,
                  alpha=alpha)
    return xt.reshape(batch, seq, d)
```

```python
import functools
import math

import jax
import jax.numpy as jnp
from jax import lax
from jax.experimental import pallas as pl
from jax.experimental.pallas import tpu as pltpu

F32 = jnp.float32
BF16 = jnp.bfloat16

CHUNK = 64
DIFF_HEAD_DIM = 128
CONV_WIDTH = 4
LRU_BLOCK_W = 256
LRU_C = 8.0
LN_EPS = 1e-5
RMS_EPS = 1e-5
NEG_BIG = -1e30
FFN_FIRST_ROWS = 512
FFN_LAST_ROWS = 256
OUTPROJ_ROW_BLOCK = 256
ATTN_ROW_BLOCK = 64
ATTN_PV_ROWS = 512
ATTN_STEPS_PER_TRIP = 4

VMEM_LIMIT_BYTES = 56 * 1024 * 1024


def _params(*sem):
    return pltpu.CompilerParams(dimension_semantics=sem, vmem_limit_bytes=VMEM_LIMIT_BYTES)


def _dot(a, b):
    return jnp.dot(a, b, preferred_element_type=F32)


def _modulate(x, shift_row, scale_row):
    return x * (1.0 + scale_row) + shift_row


def _residual_layer_norm(x, y, gate_row, res_w, alpha, ln_g, ln_b):
    z = alpha * x + (res_w * (1.0 + gate_row)) * y
    mu = jnp.mean(z, axis=-1, keepdims=True)
    zc = z - mu
    var = jnp.mean(zc * zc, axis=-1, keepdims=True)
    return zc * lax.rsqrt(var + LN_EPS) * ln_g + ln_b


def _adaln_kernel(c_ref, w_ref, b_ref, o_ref):
    c = c_ref[...]
    c_act = (c * jax.nn.sigmoid(c)).astype(BF16)
    o_ref[...] = _dot(c_act, w_ref[...].astype(BF16)) + b_ref[...]


def _adaln(c, ada_w, ada_b, *, tn=1024):
    depth, d, n = ada_w.shape
    b = c.shape[0]
    return pl.pallas_call(
        _adaln_kernel,
        grid=(depth, n // tn),
        in_specs=[
            pl.BlockSpec((b, d), lambda l, j: (0, 0)),
            pl.BlockSpec((None, d, tn), lambda l, j: (l, 0, j)),
            pl.BlockSpec((None, 1, tn), lambda l, j: (l, 0, j)),
        ],
        out_specs=pl.BlockSpec((None, b, tn), lambda l, j: (l, 0, j)),
        out_shape=jax.ShapeDtypeStruct((depth, b, n), F32),
        compiler_params=_params("parallel", "parallel"),
        name="adaln",
    )(c, ada_w, ada_b.reshape(depth, 1, n))


def _mod_spec(k, tiles_per_batch, d):
    return pl.BlockSpec((None, 1, d), lambda i, *_: ((i // tiles_per_batch) * 9 + k, 0, 0))


def _row_spec(d):
    return pl.BlockSpec((1, d), lambda *_: (0, 0))


def _ffn_kernel(x_ref, sh_ref, sc_ref, gt_ref, wa_ref, wu_ref, wo_ref, lng_ref, lnb_ref,
                o_ref, h_ref, *, alpha):
    j = pl.program_id(1)
    last = pl.num_programs(1) - 1

    def swiglu_chunk(h):
        a = _dot(h, wa_ref[...])
        u = _dot(h, wu_ref[...])
        g = (a * jax.nn.sigmoid(a) * u).astype(BF16)
        return _dot(g, wo_ref[...])

    def row_blocks(n):
        return [slice(r0, r0 + n) for r0 in range(0, x_ref.shape[0], n)]

    @pl.when(j == 0)
    def _():
        for rows in row_blocks(FFN_FIRST_ROWS):
            h = _modulate(x_ref[rows, :], sh_ref[...], sc_ref[...]).astype(BF16)
            h_ref[rows, :] = h
            o_ref[rows, :] = swiglu_chunk(h)

    @pl.when((j > 0) & (j < last))
    def _():
        o_ref[...] += swiglu_chunk(h_ref[...])

    @pl.when(j == last)
    def _():
        for rows in row_blocks(FFN_LAST_ROWS):
            y = o_ref[rows, :] + swiglu_chunk(h_ref[rows, :])
            o_ref[rows, :] = _residual_layer_norm(x_ref[rows, :], y, gt_ref[...], 0.5, alpha,
                                                  lng_ref[...], lnb_ref[...])


def _ffn(x, mod3, k0, w_in, w_out, widx, ln_g, ln_b, *, seq, alpha, tm=1024, tf=512):
    t, d = x.shape
    d_ff = w_out.shape[1]
    nf = d_ff // tf
    tpb = seq // tm
    return pl.pallas_call(
        functools.partial(_ffn_kernel, alpha=alpha),
        grid=(t // tm, nf),
        in_specs=[
            pl.BlockSpec((tm, d), lambda i, j: (i, 0)),
            _mod_spec(k0, tpb, d), _mod_spec(k0 + 1, tpb, d), _mod_spec(k0 + 2, tpb, d),
            pl.BlockSpec((None, d, tf), lambda i, j: (widx, 0, j)),
            pl.BlockSpec((None, d, tf), lambda i, j: (widx, 0, j + nf)),
            pl.BlockSpec((None, tf, d), lambda i, j: (widx, j, 0)),
            _row_spec(d), _row_spec(d),
        ],
        out_specs=pl.BlockSpec((tm, d), lambda i, j: (i, 0)),
        out_shape=jax.ShapeDtypeStruct((t, d), F32),
        scratch_shapes=[pltpu.VMEM((tm, d), BF16)],
        compiler_params=_params("parallel", "arbitrary"),
        name="ffn",
    )(x, mod3, mod3, mod3, w_in, w_in, w_out, ln_g.reshape(1, d), ln_b.reshape(1, d))


def _gelu_tanh(x):
    return 0.5 * x * (1.0 + jnp.tanh(math.sqrt(2.0 / math.pi) * (x + 0.044715 * (x * x * x))))


def _qkv_kernel(x_ref, sh_ref, sc_ref, w_ref, cs_ref, o_ref, h_ref):
    @pl.when(pl.program_id(1) == 0)
    def _():
        h_ref[...] = _modulate(x_ref[...], sh_ref[...], sc_ref[...]).astype(BF16)

    o_ref[...] = (_dot(h_ref[...], w_ref[...]) * cs_ref[...]).astype(o_ref.dtype)


def _qkv_proj(x, mod3, k0, w, col_scale, *, seq, tm=1024, tn=1536):
    t, d = x.shape
    n_out = w.shape[1]
    tpb = seq // tm
    return pl.pallas_call(
        _qkv_kernel,
        grid=(t // tm, n_out // tn),
        in_specs=[
            pl.BlockSpec((tm, d), lambda i, j: (i, 0)),
            _mod_spec(k0, tpb, d), _mod_spec(k0 + 1, tpb, d),
            pl.BlockSpec((d, tn), lambda i, j: (0, j)),
            pl.BlockSpec((1, tn), lambda i, j: (0, j)),
        ],
        out_specs=pl.BlockSpec((tm, tn), lambda i, j: (i, j)),
        out_shape=jax.ShapeDtypeStruct((t, n_out), BF16),
        scratch_shapes=[pltpu.VMEM((tm, d), BF16)],
        compiler_params=_params("parallel", "arbitrary"),
        name="qkv_proj",
    )(x, mod3, mod3, w, col_scale)


def _lru_in_kernel(x_ref, sh_ref, sc_ref, w_ref, og_ref, ox_ref, h_ref, *, n_gate_tiles):
    j = pl.program_id(1)

    @pl.when(j == 0)
    def _():
        h_ref[...] = _modulate(x_ref[...], sh_ref[...], sc_ref[...]).astype(BF16)

    @pl.when(j < n_gate_tiles)
    def _():
        og_ref[...] = _gelu_tanh(_dot(h_ref[...], w_ref[...])).astype(og_ref.dtype)

    @pl.when(j >= n_gate_tiles)
    def _():
        ox_ref[...] = _dot(h_ref[...], w_ref[...])


def _lru_in_proj(x, mod3, k0, w, *, seq, tm=1024, tn=1280):
    t, d = x.shape
    d_rnn = w.shape[1] // 2
    tpb = seq // tm
    ng = d_rnn // tn
    return pl.pallas_call(
        functools.partial(_lru_in_kernel, n_gate_tiles=ng),
        grid=(t // tm, 2 * ng),
        in_specs=[
            pl.BlockSpec((tm, d), lambda i, j: (i, 0)),
            _mod_spec(k0, tpb, d), _mod_spec(k0 + 1, tpb, d),
            pl.BlockSpec((d, tn), lambda i, j: (0, j)),
        ],
        out_specs=[pl.BlockSpec((tm, tn), lambda i, j: (i, jnp.minimum(j, ng - 1))),
                   pl.BlockSpec((tm, tn), lambda i, j: (i, jnp.maximum(j - ng, 0)))],
        out_shape=[jax.ShapeDtypeStruct((t, d_rnn), BF16), jax.ShapeDtypeStruct((t, d_rnn), F32)],
        scratch_shapes=[pltpu.VMEM((tm, d), BF16)],
        compiler_params=_params("parallel", "arbitrary"),
        name="lru_in_proj",
    )(x, mod3, mod3, w)


def _outproj_kernel(a_ref, w_ref, x_ref, gt_ref, lng_ref, lnb_ref, o_ref, *, alpha):
    for r0 in range(0, a_ref.shape[0], OUTPROJ_ROW_BLOCK):
        rows = slice(r0, r0 + OUTPROJ_ROW_BLOCK)
        y = _dot(a_ref[rows, :], w_ref[...])
        o_ref[rows, :] = _residual_layer_norm(x_ref[rows, :], y, gt_ref[...], 1.0, alpha,
                                              lng_ref[...], lnb_ref[...])


def _outproj(a, w, x, mod3, k_gate, ln_g, ln_b, *, seq, alpha, tm=512):
    t, d = x.shape
    k = a.shape[1]
    tpb = seq // tm
    return pl.pallas_call(
        functools.partial(_outproj_kernel, alpha=alpha),
        grid=(t // tm,),
        in_specs=[
            pl.BlockSpec((tm, k), lambda i: (i, 0)),
            pl.BlockSpec((k, d), lambda i: (0, 0), pipeline_mode=pl.Buffered(1)),
            pl.BlockSpec((tm, d), lambda i: (i, 0)),
            _mod_spec(k_gate, tpb, d),
            _row_spec(d), _row_spec(d),
        ],
        out_specs=pl.BlockSpec((tm, d), lambda i: (i, 0)),
        out_shape=jax.ShapeDtypeStruct((t, d), F32),
        compiler_params=_params("parallel"),
        name="outproj",
    )(a, w, x, mod3, ln_g.reshape(1, d), ln_b.reshape(1, d))


def _attn_kernel(lq1_ref, lk1_ref, lq2_ref, lk2_ref, sg_ref, q_ref, k_ref, v_ref, o_ref,
                 s0_ref, s1_ref, rm0_ref, rm1_ref, p0_ref, p1_ref, c0_ref, c1_ref,
                 m_ref, l_ref, acc_ref,
                 e_ref, bias_ref, *, tq, lambda_init):
    dh = DIFF_HEAD_DIM
    lanes = l_ref.shape[-1]
    seq = q_ref.shape[0]
    nq = seq // tq
    n_pairs = nq * (nq + 1) // 2
    assert n_pairs % ATTN_STEPS_PER_TRIP == 0 and ATTN_STEPS_PER_TRIP % 2 == 0
    assert tq // CHUNK <= lanes
    nt = (((1,), (1,)), ((), ()))

    r = lax.broadcasted_iota(jnp.int32, (tq, lanes), 0) // CHUNK
    c = lax.broadcasted_iota(jnp.int32, (tq, lanes), 1)
    e_ref[...] = jnp.where(r == c, 1.0, 0.0).astype(BF16)
    bias_ref[0] = jnp.zeros((tq, lanes), BF16)
    bias_ref[1] = jnp.where((r <= c) | (c >= tq // CHUNK), 0.0, NEG_BIG).astype(BF16)
    m_ref[...] = jnp.full(m_ref.shape, NEG_BIG, F32)
    l_ref[...] = jnp.zeros(l_ref.shape, F32)
    acc_ref[...] = jnp.zeros(acc_ref.shape, F32)

    def scores(qi, kt, slot):
        s_out, rm_out = slot[0], slot[1]
        q0 = pl.multiple_of(qi * tq, tq)
        k0 = pl.multiple_of(kt * tq, tq)
        bias = bias_ref[jnp.where(kt == qi, 1, 0)]
        e = e_ref[...]
        for s in range(2):
            qa = jnp.concatenate([q_ref[pl.ds(q0, tq), s * dh:(s + 1) * dh], e], axis=1)
            ka = jnp.concatenate([k_ref[pl.ds(k0, tq), s * dh:(s + 1) * dh], bias], axis=1)
            sc = lax.dot_general(qa, ka, nt, preferred_element_type=F32)
            s_out[s] = sc
            mx = functools.reduce(jnp.maximum,
                                  [sc[:, j * lanes:(j + 1) * lanes] for j in range(tq // lanes)])
            rm_out[s] = jnp.broadcast_to(jnp.max(mx, axis=-1, keepdims=True), (tq, lanes))

    def fold(qi, kt, slot):
        s_in, rm_in, p_buf, c_buf = slot
        q0 = pl.multiple_of(qi * tq, tq)
        v = v_ref[pl.ds(pl.multiple_of(kt * tq, tq), tq), :]
        for s, h0 in [(s, h0) for s in range(2) for h0 in range(0, tq, ATTN_PV_ROWS)]:
            for r0 in range(h0, h0 + ATTN_PV_ROWS, ATTN_ROW_BLOCK):
                rows = slice(r0, r0 + ATTN_ROW_BLOCK)
                srows = pl.ds(pl.multiple_of(q0 + r0, ATTN_ROW_BLOCK), ATTN_ROW_BLOCK)
                sc = [s_in[s, rows, j * lanes:(j + 1) * lanes] for j in range(tq // lanes)]
                m_old = m_ref[s, srows, :]
                m_new = jnp.maximum(m_old, rm_in[s, rows, :])
                corr = jnp.exp2(m_old - m_new)
                p = [jnp.exp2(x - m_new) for x in sc]
                l_ref[s, srows, :] = corr * l_ref[s, srows, :] + functools.reduce(jnp.add, p)
                m_ref[s, srows, :] = m_new
                c_buf[s, rows, :] = corr
                p_buf[s, rows, :] = jnp.concatenate(p, axis=1).astype(BF16)
            hrows = slice(h0, h0 + ATTN_PV_ROWS)
            pv = _dot(p_buf[s, hrows, :], v)
            corr = c_buf[s, hrows, :]
            qrows = pl.ds(pl.multiple_of(q0 + h0, ATTN_PV_ROWS), ATTN_PV_ROWS)
            acc_ref[s, qrows, :] = jnp.concatenate([corr, corr], axis=1) * acc_ref[s, qrows, :] + pv

    slots = ((s0_ref, rm0_ref, p0_ref, c0_ref), (s1_ref, rm1_ref, p1_ref, c1_ref))

    def step(qi, kt, cur, nxt):
        last = kt == qi
        nqi = jnp.where(last, qi + 1, qi)
        nkt = jnp.where(last, 0, kt + 1)
        scores(jnp.minimum(nqi, nq - 1), nkt, nxt)
        fold(qi, kt, cur)
        return nqi, nkt

    def trip(_, carry):
        qi, kt = carry
        for _ in range(ATTN_STEPS_PER_TRIP // 2):
            qi, kt = step(qi, kt, slots[0], slots[1])
            qi, kt = step(qi, kt, slots[1], slots[0])
        return qi, kt

    scores(0, 0, slots[0])
    lax.fori_loop(0, n_pairs // ATTN_STEPS_PER_TRIP, trip, (jnp.int32(0), jnp.int32(0)))

    lam = (jnp.exp(jnp.sum(lq1_ref[...] * lk1_ref[...], axis=-1, keepdims=True))
           - jnp.exp(jnp.sum(lq2_ref[...] * lk2_ref[...], axis=-1, keepdims=True))
           + lambda_init)
    out_scale = sg_ref[...] * (1.0 - lambda_init)

    def normalise(i, carry):
        rows = pl.ds(pl.multiple_of(i * ATTN_ROW_BLOCK, ATTN_ROW_BLOCK), ATTN_ROW_BLOCK)
        inv1 = 1.0 / jnp.sum(l_ref[0, rows, :], axis=-1, keepdims=True)
        inv2 = lam / jnp.sum(l_ref[1, rows, :], axis=-1, keepdims=True)
        o = acc_ref[0, rows, :] * inv1 - acc_ref[1, rows, :] * inv2
        o = o * lax.rsqrt(jnp.mean(o * o, axis=-1, keepdims=True) + RMS_EPS)
        o_ref[rows, :] = (o * out_scale).astype(o_ref.dtype)
        return carry

    lax.fori_loop(0, seq // ATTN_ROW_BLOCK, normalise, 0, unroll=4)


def _diff_attention(qkv, lq1, lk1, lq2, lk2, subln_g, *, batch, seq, heads, lambda_init, tq=512):
    dv = 2 * DIFF_HEAD_DIM
    lanes = 128
    vec = pl.BlockSpec((1, DIFF_HEAD_DIM), lambda b, h: (0, 0))
    return pl.pallas_call(
        functools.partial(_attn_kernel, tq=tq, lambda_init=lambda_init),
        grid=(batch, heads),
        in_specs=[
            vec, vec, vec, vec,
            pl.BlockSpec((1, dv), lambda b, h: (0, 0)),
            pl.BlockSpec((seq, dv), lambda b, h: (b, h)),
            pl.BlockSpec((seq, dv), lambda b, h: (b, heads + h)),
            pl.BlockSpec((seq, dv), lambda b, h: (b, 2 * heads + h)),
        ],
        out_specs=pl.BlockSpec((seq, dv), lambda b, h: (b, h)),
        out_shape=jax.ShapeDtypeStruct((batch * seq, heads * dv), BF16),
        scratch_shapes=[
            pltpu.VMEM((2, tq, tq), F32), pltpu.VMEM((2, tq, tq), F32),
            pltpu.VMEM((2, tq, lanes), F32), pltpu.VMEM((2, tq, lanes), F32),
            pltpu.VMEM((2, tq, tq), BF16), pltpu.VMEM((2, tq, tq), BF16),
            pltpu.VMEM((2, tq, lanes), F32), pltpu.VMEM((2, tq, lanes), F32),
            pltpu.VMEM((2, seq, lanes), F32), pltpu.VMEM((2, seq, lanes), F32),
            pltpu.VMEM((2, seq, dv), F32),
            pltpu.VMEM((tq, lanes), BF16), pltpu.VMEM((2, tq, lanes), BF16),
        ],
        compiler_params=_params("parallel", "parallel"),
        name="diff_attn",
    )(lq1.reshape(1, -1), lk1.reshape(1, -1), lq2.reshape(1, -1), lk2.reshape(1, -1),
      subln_g.reshape(1, dv), qkv, qkv, qkv)


def _lru_kernel(x_ref, g_ref, cw_ref, cb_ref, gaw_ref, gab_ref, gxw_ref, gxb_ref, lam_ref,
                o_ref, xe_ref, hc_ref, a_s, b_s, h_s, *, tt):
    t = pl.program_id(2)
    w = x_ref.shape[1]

    @pl.when(t == 0)
    def _():
        xe_ref[...] = jnp.zeros_like(xe_ref)
        hc_ref[...] = jnp.zeros_like(hc_ref)

    ng = tt // 8
    x3 = x_ref[...].reshape(ng, 8, w)
    xall = jnp.concatenate([xe_ref[...][None], x3], axis=0)
    xe_ref[...] = x3[ng - 1]
    row = lax.broadcasted_iota(jnp.int32, (ng, 8, w), 1)
    cw = cw_ref[...]
    xc = cw[CONV_WIDTH - 1:CONV_WIDTH, :] * x3 + cb_ref[...]
    for s in range(1, CONV_WIDTH):
        rot = pltpu.roll(xall, s, 1)
        shifted = jnp.where(row < s, rot[:ng], rot[1:])
        xc = xc + cw[CONV_WIDTH - 1 - s:CONV_WIDTH - s, :] * shifted

    xcb = xc.reshape(tt, w).astype(BF16)
    r = jax.nn.sigmoid(_dot(xcb, gaw_ref[...]) + gab_ref[...]).reshape(ng, 8, w)
    gi = jax.nn.sigmoid(_dot(xcb, gxw_ref[...]) + gxb_ref[...]).reshape(ng, 8, w)
    z = -lam_ref[...]
    softplus = jnp.maximum(z, 0.0) + jnp.log(1.0 + jnp.exp(-jnp.abs(z)))
    a = jnp.exp2(r * ((-LRU_C * math.log2(math.e)) * softplus))
    mult = jnp.exp2(0.5 * jnp.log2(jnp.maximum(1.0 - a * a, 0.0)))
    bx = xc * gi * mult

    for d in (1, 2, 4):
        keep = row >= d
        a_sh = jnp.where(keep, pltpu.roll(a, d, 1), 1.0)
        b_sh = jnp.where(keep, pltpu.roll(bx, d, 1), 0.0)
        bx = a * b_sh + bx
        a = a * a_sh
    a_s[...] = a.reshape(tt, w)
    b_s[...] = bx.reshape(tt, w)

    def body(g, carry):
        off = pl.multiple_of(g * 8, 8)
        h = a_s[pl.ds(off, 8), :] * carry + b_s[pl.ds(off, 8), :]
        h_s[pl.ds(off, 8), :] = h
        return h[7:8, :]

    carry = lax.fori_loop(0, tt // 8, body, hc_ref[0:1, :], unroll=8)
    hc_ref[0:1, :] = carry
    o_ref[...] = (h_s[...] * g_ref[...].astype(F32)).astype(o_ref.dtype)


def _lru_core(xb, gb, conv_w, conv_b, ga_w, ga_b, gx_w, gx_b, lam, *, batch, seq, tt=1024):
    t, d_rnn = xb.shape
    bw = LRU_BLOCK_W
    nb = d_rnn // bw
    nt = seq // tt
    tile = lambda b, n, s: (b * nt + s, n)
    row = lambda b, n, s: (0, n)
    gate_w = lambda b, n, s: (n, 0, 0)
    return pl.pallas_call(
        functools.partial(_lru_kernel, tt=tt),
        grid=(batch, nb, nt),
        in_specs=[
            pl.BlockSpec((tt, bw), tile),
            pl.BlockSpec((tt, bw), tile),
            pl.BlockSpec((CONV_WIDTH, bw), row),
            pl.BlockSpec((1, bw), row),
            pl.BlockSpec((None, bw, bw), gate_w),
            pl.BlockSpec((1, bw), row),
            pl.BlockSpec((None, bw, bw), gate_w),
            pl.BlockSpec((1, bw), row),
            pl.BlockSpec((1, bw), row),
        ],
        out_specs=pl.BlockSpec((tt, bw), tile),
        out_shape=jax.ShapeDtypeStruct((t, d_rnn), BF16),
        scratch_shapes=[
            pltpu.VMEM((8, bw), F32),
            pltpu.VMEM((8, bw), F32),
            pltpu.VMEM((tt, bw), F32),
            pltpu.VMEM((tt, bw), F32),
            pltpu.VMEM((tt, bw), F32),
        ],
        compiler_params=_params("parallel", "parallel", "arbitrary"),
        name="lru_core",
    )(xb, gb, conv_w, conv_b.reshape(1, -1), ga_w, ga_b.reshape(1, -1), gx_w, gx_b.reshape(1, -1),
      lam.reshape(1, -1))


def kernel(x, c, ada_w, ada_b, ln_g, ln_b, ffn_w_in, ffn_w_out, attn_w_qkv, attn_w_o, attn_lambda_q1, attn_lambda_k1, attn_lambda_q2, attn_lambda_k2, attn_subln_g, lru_w_in, lru_conv_w, lru_conv_b, lru_gate_a_w, lru_gate_a_b, lru_gate_x_w, lru_gate_x_b, lru_lambda, lru_w_out):
    batch, seq, d = x.shape
    depth = ada_w.shape[0]
    d_rnn = lru_w_out.shape[1]
    heads = d // (2 * DIFF_HEAD_DIM)
    alpha = (2 * depth) ** 0.25
    n_mixers = 2

    mod = _adaln(c, ada_w, ada_b)
    xt = x.reshape(batch * seq, d)
    q_scale = jnp.concatenate([jnp.full((1, d), DIFF_HEAD_DIM ** -0.5 * math.log2(math.e), F32),
                               jnp.ones((1, 2 * d), F32)], axis=1)
    ffn_in = ffn_w_in.astype(BF16).reshape((depth * 2,) + ffn_w_in.shape[2:])
    ffn_out = ffn_w_out.astype(BF16).reshape((depth * 2,) + ffn_w_out.shape[2:])

    for i in range(depth):
        mod3 = mod[i].reshape(batch * 9, 1, d)
        xt = _ffn(xt, mod3, 0, ffn_in, ffn_out, 2 * i, ln_g[i, 0], ln_b[i, 0], seq=seq, alpha=alpha)
        j = i // n_mixers
        if i % n_mixers == 0:
            lambda_init = 0.8 - 0.6 * math.exp(-0.3 * i)
            qkv = _qkv_proj(xt, mod3, 3, attn_w_qkv[j].astype(BF16), q_scale, seq=seq)
            mixed = _diff_attention(qkv, attn_lambda_q1[j], attn_lambda_k1[j], attn_lambda_q2[j],
                                    attn_lambda_k2[j], attn_subln_g[j], batch=batch, seq=seq,
                                    heads=heads, lambda_init=lambda_init)
            w_o = attn_w_o[j].astype(BF16)
        else:
            gb, xb = _lru_in_proj(xt, mod3, 3, lru_w_in[j].astype(BF16), seq=seq)
            mixed = _lru_core(xb, gb, lru_conv_w[j], lru_conv_b[j], lru_gate_a_w[j].astype(BF16),
                              lru_gate_a_b[j], lru_gate_x_w[j].astype(BF16), lru_gate_x_b[j],
                              lru_lambda[j], batch=batch, seq=seq)
            w_o = lru_w_out[j].astype(BF16)
        xt = _outproj(mixed, w_o, xt, mod3, 5, ln_g[i, 1], ln_b[i, 1], seq=seq, alpha=alpha)
        xt = _ffn(xt, mod3, 6, ffn_in, ffn_out, 2 * i + 1, ln_g[i, 2], ln_b[i, 2], seq=seq,
                  alpha=alpha)
    return xt.reshape(batch, seq, d)
```

```python
import functools
import math

import jax
import jax.numpy as jnp
from jax import lax
from jax.experimental import pallas as pl
from jax.experimental.pallas import tpu as pltpu

F32 = jnp.float32
BF16 = jnp.bfloat16

CHUNK = 64
DIFF_HEAD_DIM = 128
CONV_WIDTH = 4
LRU_BLOCK_W = 256
LRU_C = 8.0
LN_EPS = 1e-5
RMS_EPS = 1e-5
NEG_BIG = -1e30
FFN_FIRST_ROWS = 512
FFN_LAST_ROWS = 256
PROJ_FIRST_ROWS = 512
OUTPROJ_ROW_BLOCK = 256
ATTN_ROW_BLOCK = 64
ATTN_STEPS_PER_TRIP = 4

VMEM_LIMIT_BYTES = 56 * 1024 * 1024


def _params(*sem):
    return pltpu.CompilerParams(dimension_semantics=sem, vmem_limit_bytes=VMEM_LIMIT_BYTES)


def _dot(a, b):
    return jnp.dot(a, b, preferred_element_type=F32)


def _modulate(x, shift_row, scale_row):
    return x * (1.0 + scale_row) + shift_row


def _residual_layer_norm(x, y, gate_row, res_w, alpha, ln_g, ln_b):
    z = alpha * x + (res_w * (1.0 + gate_row)) * y
    mu = jnp.mean(z, axis=-1, keepdims=True)
    zc = z - mu
    var = jnp.mean(zc * zc, axis=-1, keepdims=True)
    return zc * lax.rsqrt(var + LN_EPS) * ln_g + ln_b


def _adaln_kernel(c_ref, w_ref, b_ref, o_ref):
    c = c_ref[...]
    c_act = (c * jax.nn.sigmoid(c)).astype(BF16)
    o_ref[...] = _dot(c_act, w_ref[...].astype(BF16)) + b_ref[...]


def _adaln(c, ada_w, ada_b, *, tn=1024):
    depth, d, n = ada_w.shape
    b = c.shape[0]
    return pl.pallas_call(
        _adaln_kernel,
        grid=(depth, n // tn),
        in_specs=[
            pl.BlockSpec((b, d), lambda l, j: (0, 0)),
            pl.BlockSpec((None, d, tn), lambda l, j: (l, 0, j)),
            pl.BlockSpec((None, 1, tn), lambda l, j: (l, 0, j)),
        ],
        out_specs=pl.BlockSpec((None, b, tn), lambda l, j: (l, 0, j)),
        out_shape=jax.ShapeDtypeStruct((depth, b, n), F32),
        compiler_params=_params("parallel", "parallel"),
        name="adaln",
    )(c, ada_w, ada_b.reshape(depth, 1, n))


def _mod_spec(k, tiles_per_batch, d):
    return pl.BlockSpec((None, 1, d), lambda i, *_: ((i // tiles_per_batch) * 9 + k, 0, 0))


def _row_spec(d):
    return pl.BlockSpec((1, d), lambda *_: (0, 0))


def _ffn_kernel(x_ref, sh_ref, sc_ref, gt_ref, wa_ref, wu_ref, wo_ref, lng_ref, lnb_ref,
                o_ref, h_ref, *, alpha):
    j = pl.program_id(1)
    last = pl.num_programs(1) - 1

    def swiglu_chunk(h):
        a = _dot(h, wa_ref[...])
        u = _dot(h, wu_ref[...])
        g = (a * jax.nn.sigmoid(a) * u).astype(BF16)
        return _dot(g, wo_ref[...])

    def row_blocks(n):
        return [slice(r0, r0 + n) for r0 in range(0, x_ref.shape[0], n)]

    @pl.when(j == 0)
    def _():
        for rows in row_blocks(FFN_FIRST_ROWS):
            h = _modulate(x_ref[rows, :], sh_ref[...], sc_ref[...]).astype(BF16)
            h_ref[rows, :] = h
            o_ref[rows, :] = swiglu_chunk(h)

    @pl.when((j > 0) & (j < last))
    def _():
        o_ref[...] += swiglu_chunk(h_ref[...])

    @pl.when(j == last)
    def _():
        for rows in row_blocks(FFN_LAST_ROWS):
            y = o_ref[rows, :] + swiglu_chunk(h_ref[rows, :])
            o_ref[rows, :] = _residual_layer_norm(x_ref[rows, :], y, gt_ref[...], 0.5, alpha,
                                                  lng_ref[...], lnb_ref[...])


def _ffn(x, mod3, k0, w_in, w_out, widx, ln_g, ln_b, *, seq, alpha, tm=1024, tf=512):
    t, d = x.shape
    d_ff = w_out.shape[1]
    nf = d_ff // tf
    tpb = seq // tm
    return pl.pallas_call(
        functools.partial(_ffn_kernel, alpha=alpha),
        grid=(t // tm, nf),
        in_specs=[
            pl.BlockSpec((tm, d), lambda i, j: (i, 0)),
            _mod_spec(k0, tpb, d), _mod_spec(k0 + 1, tpb, d), _mod_spec(k0 + 2, tpb, d),
            pl.BlockSpec((None, d, tf), lambda i, j: (widx, 0, j)),
            pl.BlockSpec((None, d, tf), lambda i, j: (widx, 0, j + nf)),
            pl.BlockSpec((None, tf, d), lambda i, j: (widx, j, 0)),
            _row_spec(d), _row_spec(d),
        ],
        out_specs=pl.BlockSpec((tm, d), lambda i, j: (i, 0)),
        out_shape=jax.ShapeDtypeStruct((t, d), F32),
        scratch_shapes=[pltpu.VMEM((tm, d), BF16)],
        compiler_params=_params("parallel", "arbitrary"),
        name="ffn",
    )(x, mod3, mod3, mod3, w_in, w_in, w_out, ln_g.reshape(1, d), ln_b.reshape(1, d))


def _gelu_tanh(x):
    return 0.5 * x * (1.0 + jnp.tanh(math.sqrt(2.0 / math.pi) * (x + 0.044715 * (x * x * x))))


def _modulate_rows(x_ref, sh_ref, sc_ref, h_ref, rows):
    h = _modulate(x_ref[rows, :], sh_ref[...], sc_ref[...]).astype(BF16)
    h_ref[rows, :] = h
    return h


def _row_slices(n_rows, block):
    return [slice(r0, r0 + block) for r0 in range(0, n_rows, block)]


def _qkv_kernel(x_ref, sh_ref, sc_ref, w_ref, cs_ref, o_ref, h_ref):
    j = pl.program_id(1)

    @pl.when(j == 0)
    def _():
        for rows in _row_slices(x_ref.shape[0], PROJ_FIRST_ROWS):
            h = _modulate_rows(x_ref, sh_ref, sc_ref, h_ref, rows)
            o_ref[rows, :] = (_dot(h, w_ref[...]) * cs_ref[...]).astype(o_ref.dtype)

    @pl.when(j > 0)
    def _():
        o_ref[...] = (_dot(h_ref[...], w_ref[...]) * cs_ref[...]).astype(o_ref.dtype)


def _qkv_proj(x, mod3, k0, w, col_scale, *, seq, tm=1024, tn=1536):
    t, d = x.shape
    n_out = w.shape[1]
    tpb = seq // tm
    return pl.pallas_call(
        _qkv_kernel,
        grid=(t // tm, n_out // tn),
        in_specs=[
            pl.BlockSpec((tm, d), lambda i, j: (i, 0)),
            _mod_spec(k0, tpb, d), _mod_spec(k0 + 1, tpb, d),
            pl.BlockSpec((d, tn), lambda i, j: (0, j)),
            pl.BlockSpec((1, tn), lambda i, j: (0, j)),
        ],
        out_specs=pl.BlockSpec((tm, tn), lambda i, j: (i, j)),
        out_shape=jax.ShapeDtypeStruct((t, n_out), BF16),
        scratch_shapes=[pltpu.VMEM((tm, d), BF16)],
        compiler_params=_params("parallel", "arbitrary"),
        name="qkv_proj",
    )(x, mod3, mod3, w, col_scale)


def _lru_in_kernel(x_ref, sh_ref, sc_ref, w_ref, og_ref, ox_ref, h_ref, *, n_gate_tiles):
    j = pl.program_id(1)

    @pl.when(j == 0)
    def _():
        for rows in _row_slices(x_ref.shape[0], PROJ_FIRST_ROWS):
            h = _modulate_rows(x_ref, sh_ref, sc_ref, h_ref, rows)
            og_ref[rows, :] = _gelu_tanh(_dot(h, w_ref[...])).astype(og_ref.dtype)

    @pl.when((j > 0) & (j < n_gate_tiles))
    def _():
        og_ref[...] = _gelu_tanh(_dot(h_ref[...], w_ref[...])).astype(og_ref.dtype)

    @pl.when(j >= n_gate_tiles)
    def _():
        ox_ref[...] = _dot(h_ref[...], w_ref[...])


def _lru_in_proj(x, mod3, k0, w, *, seq, tm=1024, tn=1280):
    t, d = x.shape
    d_rnn = w.shape[1] // 2
    tpb = seq // tm
    ng = d_rnn // tn
    return pl.pallas_call(
        functools.partial(_lru_in_kernel, n_gate_tiles=ng),
        grid=(t // tm, 2 * ng),
        in_specs=[
            pl.BlockSpec((tm, d), lambda i, j: (i, 0)),
            _mod_spec(k0, tpb, d), _mod_spec(k0 + 1, tpb, d),
            pl.BlockSpec((d, tn), lambda i, j: (0, j)),
        ],
        out_specs=[pl.BlockSpec((tm, tn), lambda i, j: (i, jnp.minimum(j, ng - 1))),
                   pl.BlockSpec((tm, tn), lambda i, j: (i, jnp.maximum(j - ng, 0)))],
        out_shape=[jax.ShapeDtypeStruct((t, d_rnn), BF16), jax.ShapeDtypeStruct((t, d_rnn), F32)],
        scratch_shapes=[pltpu.VMEM((tm, d), BF16)],
        compiler_params=_params("parallel", "arbitrary"),
        name="lru_in_proj",
    )(x, mod3, mod3, w)


def _outproj_kernel(a_ref, w_ref, x_ref, gt_ref, lng_ref, lnb_ref, o_ref, *, alpha):
    for r0 in range(0, a_ref.shape[0], OUTPROJ_ROW_BLOCK):
        rows = slice(r0, r0 + OUTPROJ_ROW_BLOCK)
        y = _dot(a_ref[rows, :], w_ref[...])
        o_ref[rows, :] = _residual_layer_norm(x_ref[rows, :], y, gt_ref[...], 1.0, alpha,
                                              lng_ref[...], lnb_ref[...])


def _outproj(a, w, x, mod3, k_gate, ln_g, ln_b, *, seq, alpha, tm=512):
    t, d = x.shape
    k = a.shape[1]
    tpb = seq // tm
    return pl.pallas_call(
        functools.partial(_outproj_kernel, alpha=alpha),
        grid=(t // tm,),
        in_specs=[
            pl.BlockSpec((tm, k), lambda i: (i, 0)),
            pl.BlockSpec((k, d), lambda i: (0, 0), pipeline_mode=pl.Buffered(1)),
            pl.BlockSpec((tm, d), lambda i: (i, 0)),
            _mod_spec(k_gate, tpb, d),
            _row_spec(d), _row_spec(d),
        ],
        out_specs=pl.BlockSpec((tm, d), lambda i: (i, 0)),
        out_shape=jax.ShapeDtypeStruct((t, d), F32),
        compiler_params=_params("parallel"),
        name="outproj",
    )(a, w, x, mod3, ln_g.reshape(1, d), ln_b.reshape(1, d))


def _attn_kernel(lq1_ref, lk1_ref, lq2_ref, lk2_ref, sg_ref, q_ref, k_ref, v_ref, o_ref,
                 s0_ref, rm0_ref, m_ref, l_ref, acc_ref, e_ref, bias_ref, *, tq, lambda_init):
    dh = DIFF_HEAD_DIM
    lanes = l_ref.shape[-1]
    seq = q_ref.shape[0]
    nq = seq // tq
    n_pairs = nq * (nq + 1) // 2
    assert n_pairs % ATTN_STEPS_PER_TRIP == 0 and tq // CHUNK <= lanes
    nt = (((1,), (1,)), ((), ()))

    r = lax.broadcasted_iota(jnp.int32, (tq, lanes), 0) // CHUNK
    c = lax.broadcasted_iota(jnp.int32, (tq, lanes), 1)
    e_ref[...] = jnp.where(r == c, 1.0, 0.0).astype(BF16)
    bias_ref[0] = jnp.zeros((tq, lanes), BF16)
    bias_ref[1] = jnp.where((r <= c) | (c >= tq // CHUNK), 0.0, NEG_BIG).astype(BF16)
    m_ref[...] = jnp.full(m_ref.shape, NEG_BIG, F32)
    l_ref[...] = jnp.zeros(l_ref.shape, F32)
    acc_ref[...] = jnp.zeros(acc_ref.shape, F32)

    def scores(qi, kt):
        q0 = pl.multiple_of(qi * tq, tq)
        k0 = pl.multiple_of(kt * tq, tq)
        bias = bias_ref[jnp.where(kt == qi, 1, 0)]
        e = e_ref[...]
        out = []
        for s in range(2):
            qa = jnp.concatenate([q_ref[pl.ds(q0, tq), s * dh:(s + 1) * dh], e], axis=1)
            ka = jnp.concatenate([k_ref[pl.ds(k0, tq), s * dh:(s + 1) * dh], bias], axis=1)
            sc = lax.dot_general(qa, ka, nt, preferred_element_type=F32)
            mx = functools.reduce(jnp.maximum,
                                  [sc[:, j * lanes:(j + 1) * lanes] for j in range(tq // lanes)])
            out.append((sc, jnp.broadcast_to(jnp.max(mx, axis=-1, keepdims=True), (tq, lanes))))
        return out

    def fold(qi, kt, cur):
        q0 = pl.multiple_of(qi * tq, tq)
        v = v_ref[pl.ds(pl.multiple_of(kt * tq, tq), tq), :]
        for s in range(2):
            s_in, rm_in = cur[s]
            p_blocks, c_blocks = [], []
            for r0 in range(0, tq, ATTN_ROW_BLOCK):
                rows = slice(r0, r0 + ATTN_ROW_BLOCK)
                srows = pl.ds(pl.multiple_of(q0 + r0, ATTN_ROW_BLOCK), ATTN_ROW_BLOCK)
                sc = [s_in[rows, j * lanes:(j + 1) * lanes] for j in range(tq // lanes)]
                m_old = m_ref[s, srows, :]
                m_new = jnp.maximum(m_old, rm_in[rows, :])
                corr = jnp.exp2(m_old - m_new)
                p = [jnp.exp2(x - m_new) for x in sc]
                l_ref[s, srows, :] = corr * l_ref[s, srows, :] + functools.reduce(jnp.add, p)
                m_ref[s, srows, :] = m_new
                c_blocks.append(corr)
                p_blocks.append(jnp.concatenate(p, axis=1).astype(BF16))
            pv = _dot(jnp.concatenate(p_blocks, axis=0), v)
            corr = jnp.concatenate(c_blocks, axis=0)
            qrows = pl.ds(q0, tq)
            acc_ref[s, qrows, :] = jnp.concatenate([corr, corr], axis=1) * acc_ref[s, qrows, :] + pv

    def stash(cur):
        for s in range(2):
            s0_ref[s], rm0_ref[s] = cur[s]

    def trip(_, carry):
        qi, kt = carry
        cur = [(s0_ref[s], rm0_ref[s]) for s in range(2)]
        for _ in range(ATTN_STEPS_PER_TRIP):
            last = kt == qi
            nqi = jnp.where(last, qi + 1, qi)
            nkt = jnp.where(last, 0, kt + 1)
            nxt = scores(jnp.minimum(nqi, nq - 1), nkt)
            fold(qi, kt, cur)
            cur, qi, kt = nxt, nqi, nkt
        stash(cur)
        return qi, kt

    stash(scores(0, 0))
    lax.fori_loop(0, n_pairs // ATTN_STEPS_PER_TRIP, trip, (jnp.int32(0), jnp.int32(0)))

    lam = (jnp.exp(jnp.sum(lq1_ref[...] * lk1_ref[...], axis=-1, keepdims=True))
           - jnp.exp(jnp.sum(lq2_ref[...] * lk2_ref[...], axis=-1, keepdims=True))
           + lambda_init)
    out_scale = sg_ref[...] * (1.0 - lambda_init)

    def normalise(i, carry):
        rows = pl.ds(pl.multiple_of(i * ATTN_ROW_BLOCK, ATTN_ROW_BLOCK), ATTN_ROW_BLOCK)
        inv1 = 1.0 / jnp.sum(l_ref[0, rows, :], axis=-1, keepdims=True)
        inv2 = lam / jnp.sum(l_ref[1, rows, :], axis=-1, keepdims=True)
        o = acc_ref[0, rows, :] * inv1 - acc_ref[1, rows, :] * inv2
        o = o * lax.rsqrt(jnp.mean(o * o, axis=-1, keepdims=True) + RMS_EPS)
        o_ref[rows, :] = (o * out_scale).astype(o_ref.dtype)
        return carry

    lax.fori_loop(0, seq // ATTN_ROW_BLOCK, normalise, 0, unroll=8)


def _diff_attention(qkv, lq1, lk1, lq2, lk2, subln_g, *, batch, seq, heads, lambda_init, tq=512):
    dv = 2 * DIFF_HEAD_DIM
    lanes = 128
    vec = pl.BlockSpec((1, DIFF_HEAD_DIM), lambda b, h: (0, 0))
    return pl.pallas_call(
        functools.partial(_attn_kernel, tq=tq, lambda_init=lambda_init),
        grid=(batch, heads),
        in_specs=[
            vec, vec, vec, vec,
            pl.BlockSpec((1, dv), lambda b, h: (0, 0)),
            pl.BlockSpec((seq, dv), lambda b, h: (b, h)),
            pl.BlockSpec((seq, dv), lambda b, h: (b, heads + h)),
            pl.BlockSpec((seq, dv), lambda b, h: (b, 2 * heads + h)),
        ],
        out_specs=pl.BlockSpec((seq, dv), lambda b, h: (b, h)),
        out_shape=jax.ShapeDtypeStruct((batch * seq, heads * dv), BF16),
        scratch_shapes=[
            pltpu.VMEM((2, tq, tq), F32), pltpu.VMEM((2, tq, lanes), F32),
            pltpu.VMEM((2, seq, lanes), F32), pltpu.VMEM((2, seq, lanes), F32),
            pltpu.VMEM((2, seq, dv), F32),
            pltpu.VMEM((tq, lanes), BF16), pltpu.VMEM((2, tq, lanes), BF16),
        ],
        compiler_params=_params("parallel", "parallel"),
        name="diff_attn",
    )(lq1.reshape(1, -1), lk1.reshape(1, -1), lq2.reshape(1, -1), lk2.reshape(1, -1),
      subln_g.reshape(1, dv), qkv, qkv, qkv)


def _lru_kernel(x_ref, g_ref, cw_ref, cb_ref, gaw_ref, gab_ref, gxw_ref, gxb_ref, lam_ref,
                o_ref, xe_ref, hc_ref, a_s, b_s, h_s, *, tt):
    t = pl.program_id(2)
    w = x_ref.shape[1]

    @pl.when(t == 0)
    def _():
        xe_ref[...] = jnp.zeros_like(xe_ref)
        hc_ref[...] = jnp.zeros_like(hc_ref)

    ng = tt // 8
    x3 = x_ref[...].reshape(ng, 8, w)
    xall = jnp.concatenate([xe_ref[...][None], x3], axis=0)
    xe_ref[...] = x3[ng - 1]
    row = lax.broadcasted_iota(jnp.int32, (ng, 8, w), 1)
    cw = cw_ref[...]
    xc = cw[CONV_WIDTH - 1:CONV_WIDTH, :] * x3 + cb_ref[...]
    for s in range(1, CONV_WIDTH):
        rot = pltpu.roll(xall, s, 1)
        shifted = jnp.where(row < s, rot[:ng], rot[1:])
        xc = xc + cw[CONV_WIDTH - 1 - s:CONV_WIDTH - s, :] * shifted

    xcb = xc.reshape(tt, w).astype(BF16)
    r = jax.nn.sigmoid(_dot(xcb, gaw_ref[...]) + gab_ref[...]).reshape(ng, 8, w)
    gi = jax.nn.sigmoid(_dot(xcb, gxw_ref[...]) + gxb_ref[...]).reshape(ng, 8, w)
    z = -lam_ref[...]
    softplus = jnp.maximum(z, 0.0) + jnp.log(1.0 + jnp.exp(-jnp.abs(z)))
    a = jnp.exp2(r * ((-LRU_C * math.log2(math.e)) * softplus))
    mult = jnp.exp2(0.5 * jnp.log2(jnp.maximum(1.0 - a * a, 0.0)))
    bx = xc * gi * mult

    for d in (1, 2, 4):
        keep = row >= d
        a_sh = jnp.where(keep, pltpu.roll(a, d, 1), 1.0)
        b_sh = jnp.where(keep, pltpu.roll(bx, d, 1), 0.0)
        bx = a * b_sh + bx
        a = a * a_sh
    a_s[...] = a.reshape(tt, w)
    b_s[...] = bx.reshape(tt, w)

    def body(g, carry):
        off = pl.multiple_of(g * 8, 8)
        h = a_s[pl.ds(off, 8), :] * carry + b_s[pl.ds(off, 8), :]
        h_s[pl.ds(off, 8), :] = h
        return h[7:8, :]

    carry = lax.fori_loop(0, tt // 8, body, hc_ref[0:1, :], unroll=8)
    hc_ref[0:1, :] = carry
    o_ref[...] = (h_s[...] * g_ref[...].astype(F32)).astype(o_ref.dtype)


def _lru_core(xb, gb, conv_w, conv_b, ga_w, ga_b, gx_w, gx_b, lam, *, batch, seq, tt=2048):
    t, d_rnn = xb.shape
    bw = LRU_BLOCK_W
    nb = d_rnn // bw
    nt = seq // tt
    tile = lambda b, n, s: (b * nt + s, n)
    row = lambda b, n, s: (0, n)
    gate_w = lambda b, n, s: (n, 0, 0)
    return pl.pallas_call(
        functools.partial(_lru_kernel, tt=tt),
        grid=(batch, nb, nt),
        in_specs=[
            pl.BlockSpec((tt, bw), tile),
            pl.BlockSpec((tt, bw), tile),
            pl.BlockSpec((CONV_WIDTH, bw), row),
            pl.BlockSpec((1, bw), row),
            pl.BlockSpec((None, bw, bw), gate_w),
            pl.BlockSpec((1, bw), row),
            pl.BlockSpec((None, bw, bw), gate_w),
            pl.BlockSpec((1, bw), row),
            pl.BlockSpec((1, bw), row),
        ],
        out_specs=pl.BlockSpec((tt, bw), tile),
        out_shape=jax.ShapeDtypeStruct((t, d_rnn), BF16),
        scratch_shapes=[
            pltpu.VMEM((8, bw), F32),
            pltpu.VMEM((8, bw), F32),
            pltpu.VMEM((tt, bw), F32),
            pltpu.VMEM((tt, bw), F32),
            pltpu.VMEM((tt, bw), F32),
        ],
        compiler_params=_params("parallel", "parallel", "arbitrary"),
        name="lru_core",
    )(xb, gb, conv_w, conv_b.reshape(1, -1), ga_w, ga_b.reshape(1, -1), gx_w, gx_b.reshape(1, -1),
      lam.reshape(1, -1))


def kernel(x, c, ada_w, ada_b, ln_g, ln_b, ffn_w_in, ffn_w_out, attn_w_qkv, attn_w_o, attn_lambda_q1, attn_lambda_k1, attn_lambda_q2, attn_lambda_k2, attn_subln_g, lru_w_in, lru_conv_w, lru_conv_b, lru_gate_a_w, lru_gate_a_b, lru_gate_x_w, lru_gate_x_b, lru_lambda, lru_w_out):
    batch, seq, d = x.shape
    depth = ada_w.shape[0]
    d_rnn = lru_w_out.shape[1]
    heads = d // (2 * DIFF_HEAD_DIM)
    alpha = (2 * depth) ** 0.25
    n_mixers = 2

    mod = _adaln(c, ada_w, ada_b)
    xt = x.reshape(batch * seq, d)
    q_scale = jnp.concatenate([jnp.full((1, d), DIFF_HEAD_DIM ** -0.5 * math.log2(math.e), F32),
                               jnp.ones((1, 2 * d), F32)], axis=1)
    ffn_in = ffn_w_in.astype(BF16).reshape((depth * 2,) + ffn_w_in.shape[2:])
    ffn_out = ffn_w_out.astype(BF16).reshape((depth * 2,) + ffn_w_out.shape[2:])

    for i in range(depth):
        mod3 = mod[i].reshape(batch * 9, 1, d)
        xt = _ffn(xt, mod3, 0, ffn_in, ffn_out, 2 * i, ln_g[i, 0], ln_b[i, 0], seq=seq, alpha=alpha)
        j = i // n_mixers
        if i % n_mixers == 0:
            lambda_init = 0.8 - 0.6 * math.exp(-0.3 * i)
            qkv = _qkv_proj(xt, mod3, 3, attn_w_qkv[j].astype(BF16), q_scale, seq=seq)
            mixed = _diff_attention(qkv, attn_lambda_q1[j], attn_lambda_k1[j], attn_lambda_q2[j],
                                    attn_lambda_k2[j], attn_subln_g[j], batch=batch, seq=seq,
                                    heads=heads, lambda_init=lambda_init)
            w_o = attn_w_o[j].astype(BF16)
        else:
            gb, xb = _lru_in_proj(xt, mod3, 3, lru_w_in[j].astype(BF16), seq=seq)
            mixed = _lru_core(xb, gb, lru_conv_w[j], lru_conv_b[j], lru_gate_a_w[j].astype(BF16),
                              lru_gate_a_b[j], lru_gate_x_w[j].astype(BF16), lru_gate_x_b[j],
                              lru_lambda[j], batch=batch, seq=seq)
            w_o = lru_w_out[j].astype(BF16)
        xt = _outproj(mixed, w_o, xt, mod3, 5, ln_g[i, 1], ln_b[i, 1], seq=seq, alpha=alpha)
        xt = _ffn(xt, mod3, 6, ffn_in, ffn_out, 2 * i + 1, ln_g[i, 2], ln_b[i, 2], seq=seq,
                  alpha=alpha)
    return xt.reshape(batch, seq, d)
```

```python
import functools
import math

import jax
import jax.numpy as jnp
from jax import lax
from jax.experimental import pallas as pl
from jax.experimental.pallas import tpu as pltpu

F32 = jnp.float32
BF16 = jnp.bfloat16

CHUNK = 64
DIFF_HEAD_DIM = 128
CONV_WIDTH = 4
LRU_BLOCK_W = 256
LRU_C = 8.0
LN_EPS = 1e-5
RMS_EPS = 1e-5
SUBLAYERS = 3
MOD_ROWS = 3
SHIFT, SCALE, GATE = 0, 1, 2
NEG_BIG = -1e30
FFN_FIRST_ROWS = 512
FFN_LAST_ROWS = 256
PROJ_FIRST_ROWS = 512
OUTPROJ_ROW_BLOCK = 256
ATTN_ROW_BLOCK = 64
ATTN_STEPS_PER_TRIP = 4

VMEM_LIMIT_BYTES = 56 * 1024 * 1024


def _params(*sem):
    return pltpu.CompilerParams(dimension_semantics=sem, vmem_limit_bytes=VMEM_LIMIT_BYTES)


def _dot(a, b):
    return jnp.dot(a, b, preferred_element_type=F32)


def _modulate(x, shift_row, scale_row):
    return x * (1.0 + scale_row) + shift_row


def _residual_layer_norm(x, y, gate_row, res_w, alpha, ln_g, ln_b):
    z = alpha * x + (res_w * (1.0 + gate_row)) * y
    mu = jnp.mean(z, axis=-1, keepdims=True)
    zc = z - mu
    var = jnp.mean(zc * zc, axis=-1, keepdims=True)
    return zc * lax.rsqrt(var + LN_EPS) * ln_g + ln_b


def _adaln_kernel(c_ref, w_ref, b_ref, o_ref):
    c = c_ref[...]
    c_act = (c * jax.nn.sigmoid(c)).astype(BF16)
    o_ref[...] = _dot(c_act, w_ref[...].astype(BF16)) + b_ref[...]


def _adaln(c, ada_w, ada_b, *, tn=1024):
    depth, d, n = ada_w.shape
    b = c.shape[0]
    return pl.pallas_call(
        _adaln_kernel,
        grid=(depth, n // tn),
        in_specs=[
            pl.BlockSpec((b, d), lambda l, j: (0, 0)),
            pl.BlockSpec((None, d, tn), lambda l, j: (l, 0, j)),
            pl.BlockSpec((None, 1, tn), lambda l, j: (l, 0, j)),
        ],
        out_specs=pl.BlockSpec((None, b, tn), lambda l, j: (l, 0, j)),
        out_shape=jax.ShapeDtypeStruct((depth, b, n), F32),
        compiler_params=_params("parallel", "parallel"),
        name="adaln",
    )(c, ada_w, ada_b.reshape(depth, 1, n))


def _mod_spec(sub, tiles_per_batch, d):
    return pl.BlockSpec((None, MOD_ROWS, d),
                        lambda i, *_: ((i // tiles_per_batch) * SUBLAYERS + sub, 0, 0))


def _ln_spec(d):
    return pl.BlockSpec((2, d), lambda *_: (0, 0))


def _row_slices(n_rows, block):
    return [slice(r0, r0 + block) for r0 in range(0, n_rows, block)]


def _modulate_rows(x_ref, md_ref, h_ref, rows):
    h = _modulate(x_ref[rows, :], md_ref[SHIFT:SHIFT + 1, :], md_ref[SCALE:SCALE + 1, :])
    h = h.astype(BF16)
    h_ref[rows, :] = h
    return h


def _ffn_kernel(x_ref, md_ref, wa_ref, wu_ref, wo_ref, ln_ref, o_ref, h_ref, *, alpha):
    j = pl.program_id(1)
    last = pl.num_programs(1) - 1

    def swiglu_chunk(h):
        a = _dot(h, wa_ref[...])
        u = _dot(h, wu_ref[...])
        g = (a * jax.nn.sigmoid(a) * u).astype(BF16)
        return _dot(g, wo_ref[...])

    @pl.when(j == 0)
    def _():
        for rows in _row_slices(x_ref.shape[0], FFN_FIRST_ROWS):
            o_ref[rows, :] = swiglu_chunk(_modulate_rows(x_ref, md_ref, h_ref, rows))

    @pl.when((j > 0) & (j < last))
    def _():
        o_ref[...] += swiglu_chunk(h_ref[...])

    @pl.when(j == last)
    def _():
        for rows in _row_slices(x_ref.shape[0], FFN_LAST_ROWS):
            y = o_ref[rows, :] + swiglu_chunk(h_ref[rows, :])
            o_ref[rows, :] = _residual_layer_norm(x_ref[rows, :], y, md_ref[GATE:GATE + 1, :], 0.5,
                                                  alpha, ln_ref[0:1, :], ln_ref[1:2, :])


def _ffn(x, mod3, sub, w_in, w_out, widx, ln, *, seq, alpha, tm=1024, tf=512):
    t, d = x.shape
    d_ff = w_out.shape[1]
    nf = d_ff // tf
    tpb = seq // tm
    return pl.pallas_call(
        functools.partial(_ffn_kernel, alpha=alpha),
        grid=(t // tm, nf),
        in_specs=[
            pl.BlockSpec((tm, d), lambda i, j: (i, 0)),
            _mod_spec(sub, tpb, d),
            pl.BlockSpec((None, d, tf), lambda i, j: (widx, 0, j)),
            pl.BlockSpec((None, d, tf), lambda i, j: (widx, 0, j + nf)),
            pl.BlockSpec((None, tf, d), lambda i, j: (widx, j, 0)),
            _ln_spec(d),
        ],
        out_specs=pl.BlockSpec((tm, d), lambda i, j: (i, 0)),
        out_shape=jax.ShapeDtypeStruct((t, d), F32),
        scratch_shapes=[pltpu.VMEM((tm, d), BF16)],
        compiler_params=_params("parallel", "arbitrary"),
        name="ffn",
    )(x, mod3, w_in, w_in, w_out, ln)


def _gelu_tanh(x):
    return 0.5 * x * (1.0 + jnp.tanh(math.sqrt(2.0 / math.pi) * (x + 0.044715 * (x * x * x))))


def _qkv_kernel(x_ref, md_ref, w_ref, cs_ref, o_ref, h_ref):
    j = pl.program_id(1)

    @pl.when(j == 0)
    def _():
        for rows in _row_slices(x_ref.shape[0], PROJ_FIRST_ROWS):
            h = _modulate_rows(x_ref, md_ref, h_ref, rows)
            o_ref[rows, :] = (_dot(h, w_ref[...]) * cs_ref[...]).astype(o_ref.dtype)

    @pl.when(j > 0)
    def _():
        o_ref[...] = (_dot(h_ref[...], w_ref[...]) * cs_ref[...]).astype(o_ref.dtype)


def _qkv_proj(x, mod3, sub, w, col_scale, *, seq, tm=1024, tn=1536):
    t, d = x.shape
    n_out = w.shape[1]
    tpb = seq // tm
    return pl.pallas_call(
        _qkv_kernel,
        grid=(t // tm, n_out // tn),
        in_specs=[
            pl.BlockSpec((tm, d), lambda i, j: (i, 0)),
            _mod_spec(sub, tpb, d),
            pl.BlockSpec((d, tn), lambda i, j: (0, j)),
            pl.BlockSpec((1, tn), lambda i, j: (0, j)),
        ],
        out_specs=pl.BlockSpec((tm, tn), lambda i, j: (i, j)),
        out_shape=jax.ShapeDtypeStruct((t, n_out), BF16),
        scratch_shapes=[pltpu.VMEM((tm, d), BF16)],
        compiler_params=_params("parallel", "arbitrary"),
        name="qkv_proj",
    )(x, mod3, w, col_scale)


def _lru_in_kernel(x_ref, md_ref, w_ref, og_ref, ox_ref, h_ref, *, n_gate_tiles):
    j = pl.program_id(1)

    @pl.when(j == 0)
    def _():
        for rows in _row_slices(x_ref.shape[0], PROJ_FIRST_ROWS):
            h = _modulate_rows(x_ref, md_ref, h_ref, rows)
            og_ref[rows, :] = _gelu_tanh(_dot(h, w_ref[...])).astype(og_ref.dtype)

    @pl.when((j > 0) & (j < n_gate_tiles))
    def _():
        og_ref[...] = _gelu_tanh(_dot(h_ref[...], w_ref[...])).astype(og_ref.dtype)

    @pl.when(j >= n_gate_tiles)
    def _():
        ox_ref[...] = _dot(h_ref[...], w_ref[...])


def _lru_in_proj(x, mod3, sub, w, *, seq, tm=1024, tn=1280):
    t, d = x.shape
    d_rnn = w.shape[1] // 2
    tpb = seq // tm
    ng = d_rnn // tn
    return pl.pallas_call(
        functools.partial(_lru_in_kernel, n_gate_tiles=ng),
        grid=(t // tm, 2 * ng),
        in_specs=[
            pl.BlockSpec((tm, d), lambda i, j: (i, 0)),
            _mod_spec(sub, tpb, d),
            pl.BlockSpec((d, tn), lambda i, j: (0, j)),
        ],
        out_specs=[pl.BlockSpec((tm, tn), lambda i, j: (i, jnp.minimum(j, ng - 1))),
                   pl.BlockSpec((tm, tn), lambda i, j: (i, jnp.maximum(j - ng, 0)))],
        out_shape=[jax.ShapeDtypeStruct((t, d_rnn), BF16), jax.ShapeDtypeStruct((t, d_rnn), F32)],
        scratch_shapes=[pltpu.VMEM((tm, d), BF16)],
        compiler_params=_params("parallel", "arbitrary"),
        name="lru_in_proj",
    )(x, mod3, w)


def _outproj_kernel(a_ref, w_ref, x_ref, md_ref, ln_ref, o_ref, *, alpha):
    for r0 in range(0, a_ref.shape[0], OUTPROJ_ROW_BLOCK):
        rows = slice(r0, r0 + OUTPROJ_ROW_BLOCK)
        y = _dot(a_ref[rows, :], w_ref[...])
        o_ref[rows, :] = _residual_layer_norm(x_ref[rows, :], y, md_ref[GATE:GATE + 1, :], 1.0,
                                              alpha, ln_ref[0:1, :], ln_ref[1:2, :])


def _outproj(a, w, x, mod3, sub, ln, *, seq, alpha, tm=512):
    t, d = x.shape
    k = a.shape[1]
    tpb = seq // tm
    return pl.pallas_call(
        functools.partial(_outproj_kernel, alpha=alpha),
        grid=(t // tm,),
        in_specs=[
            pl.BlockSpec((tm, k), lambda i: (i, 0)),
            pl.BlockSpec((k, d), lambda i: (0, 0), pipeline_mode=pl.Buffered(1)),
            pl.BlockSpec((tm, d), lambda i: (i, 0)),
            _mod_spec(sub, tpb, d),
            _ln_spec(d),
        ],
        out_specs=pl.BlockSpec((tm, d), lambda i: (i, 0)),
        out_shape=jax.ShapeDtypeStruct((t, d), F32),
        compiler_params=_params("parallel"),
        name="outproj",
    )(a, w, x, mod3, ln)


def _attn_kernel(lq1_ref, lk1_ref, lq2_ref, lk2_ref, sg_ref, q_ref, k_ref, v_ref, o_ref,
                 s0_ref, rm0_ref, m_ref, l_ref, acc_ref, e_ref, bias_ref, *, tq, lambda_init):
    dh = DIFF_HEAD_DIM
    lanes = l_ref.shape[-1]
    seq = q_ref.shape[0]
    nq = seq // tq
    n_pairs = nq * (nq + 1) // 2
    assert n_pairs % ATTN_STEPS_PER_TRIP == 0 and tq // CHUNK <= lanes
    nt = (((1,), (1,)), ((), ()))

    r = lax.broadcasted_iota(jnp.int32, (tq, lanes), 0) // CHUNK
    c = lax.broadcasted_iota(jnp.int32, (tq, lanes), 1)
    e_ref[...] = jnp.where(r == c, 1.0, 0.0).astype(BF16)
    bias_ref[0] = jnp.zeros((tq, lanes), BF16)
    bias_ref[1] = jnp.where((r <= c) | (c >= tq // CHUNK), 0.0, NEG_BIG).astype(BF16)
    m_ref[...] = jnp.full(m_ref.shape, NEG_BIG, F32)
    l_ref[...] = jnp.zeros(l_ref.shape, F32)
    acc_ref[...] = jnp.zeros(acc_ref.shape, F32)

    def scores(qi, kt):
        q0 = pl.multiple_of(qi * tq, tq)
        k0 = pl.multiple_of(kt * tq, tq)
        bias = bias_ref[jnp.where(kt == qi, 1, 0)]
        e = e_ref[...]
        out = []
        for s in range(2):
            qa = jnp.concatenate([q_ref[pl.ds(q0, tq), s * dh:(s + 1) * dh], e], axis=1)
            ka = jnp.concatenate([k_ref[pl.ds(k0, tq), s * dh:(s + 1) * dh], bias], axis=1)
            sc = lax.dot_general(qa, ka, nt, preferred_element_type=F32)
            mx = functools.reduce(jnp.maximum,
                                  [sc[:, j * lanes:(j + 1) * lanes] for j in range(tq // lanes)])
            out.append((sc, jnp.broadcast_to(jnp.max(mx, axis=-1, keepdims=True), (tq, lanes))))
        return out

    def fold(qi, kt, cur):
        q0 = pl.multiple_of(qi * tq, tq)
        v = v_ref[pl.ds(pl.multiple_of(kt * tq, tq), tq), :]
        for s in range(2):
            s_in, rm_in = cur[s]
            p_blocks, c_blocks = [], []
            for r0 in range(0, tq, ATTN_ROW_BLOCK):
                rows = slice(r0, r0 + ATTN_ROW_BLOCK)
                srows = pl.ds(pl.multiple_of(q0 + r0, ATTN_ROW_BLOCK), ATTN_ROW_BLOCK)
                sc = [s_in[rows, j * lanes:(j + 1) * lanes] for j in range(tq // lanes)]
                m_old = m_ref[s, srows, :]
                m_new = jnp.maximum(m_old, rm_in[rows, :])
                corr = jnp.exp2(m_old - m_new)
                p = [jnp.exp2(x - m_new) for x in sc]
                l_ref[s, srows, :] = corr * l_ref[s, srows, :] + functools.reduce(jnp.add, p)
                m_ref[s, srows, :] = m_new
                c_blocks.append(corr)
                p_blocks.append(jnp.concatenate(p, axis=1).astype(BF16))
            pv = _dot(jnp.concatenate(p_blocks, axis=0), v)
            corr = jnp.concatenate(c_blocks, axis=0)
            qrows = pl.ds(q0, tq)
            acc_ref[s, qrows, :] = jnp.concatenate([corr, corr], axis=1) * acc_ref[s, qrows, :] + pv

    def stash(cur):
        for s in range(2):
            s0_ref[s], rm0_ref[s] = cur[s]

    def trip(_, carry):
        qi, kt = carry
        cur = [(s0_ref[s], rm0_ref[s]) for s in range(2)]
        for _ in range(ATTN_STEPS_PER_TRIP):
            last = kt == qi
            nqi = jnp.where(last, qi + 1, qi)
            nkt = jnp.where(last, 0, kt + 1)
            nxt = scores(jnp.minimum(nqi, nq - 1), nkt)
            fold(qi, kt, cur)
            cur, qi, kt = nxt, nqi, nkt
        stash(cur)
        return qi, kt

    stash(scores(0, 0))
    lax.fori_loop(0, n_pairs // ATTN_STEPS_PER_TRIP, trip, (jnp.int32(0), jnp.int32(0)))

    lam = (jnp.exp(jnp.sum(lq1_ref[...] * lk1_ref[...], axis=-1, keepdims=True))
           - jnp.exp(jnp.sum(lq2_ref[...] * lk2_ref[...], axis=-1, keepdims=True))
           + lambda_init)
    out_scale = sg_ref[...] * (1.0 - lambda_init)

    def normalise(i, carry):
        rows = pl.ds(pl.multiple_of(i * ATTN_ROW_BLOCK, ATTN_ROW_BLOCK), ATTN_ROW_BLOCK)
        inv1 = 1.0 / jnp.sum(l_ref[0, rows, :], axis=-1, keepdims=True)
        inv2 = lam / jnp.sum(l_ref[1, rows, :], axis=-1, keepdims=True)
        o = acc_ref[0, rows, :] * inv1 - acc_ref[1, rows, :] * inv2
        o = o * lax.rsqrt(jnp.mean(o * o, axis=-1, keepdims=True) + RMS_EPS)
        o_ref[rows, :] = (o * out_scale).astype(o_ref.dtype)
        return carry

    lax.fori_loop(0, seq // ATTN_ROW_BLOCK, normalise, 0, unroll=8)


def _diff_attention(qkv, lq1, lk1, lq2, lk2, subln_g, *, batch, seq, heads, lambda_init, tq=512):
    dv = 2 * DIFF_HEAD_DIM
    lanes = 128
    vec = pl.BlockSpec((1, DIFF_HEAD_DIM), lambda b, h: (0, 0))
    return pl.pallas_call(
        functools.partial(_attn_kernel, tq=tq, lambda_init=lambda_init),
        grid=(batch, heads),
        in_specs=[
            vec, vec, vec, vec,
            pl.BlockSpec((1, dv), lambda b, h: (0, 0)),
            pl.BlockSpec((seq, dv), lambda b, h: (b, h)),
            pl.BlockSpec((seq, dv), lambda b, h: (b, heads + h)),
            pl.BlockSpec((seq, dv), lambda b, h: (b, 2 * heads + h)),
        ],
        out_specs=pl.BlockSpec((seq, dv), lambda b, h: (b, h)),
        out_shape=jax.ShapeDtypeStruct((batch * seq, heads * dv), BF16),
        scratch_shapes=[
            pltpu.VMEM((2, tq, tq), F32), pltpu.VMEM((2, tq, lanes), F32),
            pltpu.VMEM((2, seq, lanes), F32), pltpu.VMEM((2, seq, lanes), F32),
            pltpu.VMEM((2, seq, dv), F32),
            pltpu.VMEM((tq, lanes), BF16), pltpu.VMEM((2, tq, lanes), BF16),
        ],
        compiler_params=_params("parallel", "parallel"),
        name="diff_attn",
    )(lq1.reshape(1, -1), lk1.reshape(1, -1), lq2.reshape(1, -1), lk2.reshape(1, -1),
      subln_g.reshape(1, dv), qkv, qkv, qkv)


def _lru_kernel(x_ref, g_ref, cw_ref, cb_ref, gaw_ref, gab_ref, gxw_ref, gxb_ref, lam_ref,
                o_ref, xe_ref, hc_ref, a_s, b_s, h_s, *, tt):
    t = pl.program_id(2)
    w = x_ref.shape[1]

    @pl.when(t == 0)
    def _():
        xe_ref[...] = jnp.zeros_like(xe_ref)
        hc_ref[...] = jnp.zeros_like(hc_ref)

    ng = tt // 8
    x3 = x_ref[...].reshape(ng, 8, w)
    xall = jnp.concatenate([xe_ref[...][None], x3], axis=0)
    xe_ref[...] = x3[ng - 1]
    row = lax.broadcasted_iota(jnp.int32, (ng, 8, w), 1)
    cw = cw_ref[...]
    xc = cw[CONV_WIDTH - 1:CONV_WIDTH, :] * x3 + cb_ref[...]
    for s in range(1, CONV_WIDTH):
        rot = pltpu.roll(xall, s, 1)
        shifted = jnp.where(row < s, rot[:ng], rot[1:])
        xc = xc + cw[CONV_WIDTH - 1 - s:CONV_WIDTH - s, :] * shifted

    xcb = xc.reshape(tt, w).astype(BF16)
    r = jax.nn.sigmoid(_dot(xcb, gaw_ref[...]) + gab_ref[...]).reshape(ng, 8, w)
    gi = jax.nn.sigmoid(_dot(xcb, gxw_ref[...]) + gxb_ref[...]).reshape(ng, 8, w)
    z = -lam_ref[...]
    softplus = jnp.maximum(z, 0.0) + jnp.log(1.0 + jnp.exp(-jnp.abs(z)))
    a = jnp.exp2(r * ((-LRU_C * math.log2(math.e)) * softplus))
    mult = jnp.exp2(0.5 * jnp.log2(jnp.maximum(1.0 - a * a, 0.0)))
    bx = xc * gi * mult

    for d in (1, 2, 4):
        keep = row >= d
        a_sh = jnp.where(keep, pltpu.roll(a, d, 1), 1.0)
        b_sh = jnp.where(keep, pltpu.roll(bx, d, 1), 0.0)
        bx = a * b_sh + bx
        a = a * a_sh
    a_s[...] = a.reshape(tt, w)
    b_s[...] = bx.reshape(tt, w)

    def body(g, carry):
        off = pl.multiple_of(g * 8, 8)
        h = a_s[pl.ds(off, 8), :] * carry + b_s[pl.ds(off, 8), :]
        h_s[pl.ds(off, 8), :] = h
        return h[7:8, :]

    carry = lax.fori_loop(0, tt // 8, body, hc_ref[0:1, :], unroll=8)
    hc_ref[0:1, :] = carry
    o_ref[...] = (h_s[...] * g_ref[...].astype(F32)).astype(o_ref.dtype)


def _lru_core(xb, gb, conv_w, conv_b, ga_w, ga_b, gx_w, gx_b, lam, *, batch, seq, tt=2048):
    t, d_rnn = xb.shape
    bw = LRU_BLOCK_W
    nb = d_rnn // bw
    nt = seq // tt
    tile = lambda b, n, s: (b * nt + s, n)
    row = lambda b, n, s: (0, n)
    gate_w = lambda b, n, s: (n, 0, 0)
    return pl.pallas_call(
        functools.partial(_lru_kernel, tt=tt),
        grid=(batch, nb, nt),
        in_specs=[
            pl.BlockSpec((tt, bw), tile),
            pl.BlockSpec((tt, bw), tile),
            pl.BlockSpec((CONV_WIDTH, bw), row),
            pl.BlockSpec((1, bw), row),
            pl.BlockSpec((None, bw, bw), gate_w),
            pl.BlockSpec((1, bw), row),
            pl.BlockSpec((None, bw, bw), gate_w),
            pl.BlockSpec((1, bw), row),
            pl.BlockSpec((1, bw), row),
        ],
        out_specs=pl.BlockSpec((tt, bw), tile),
        out_shape=jax.ShapeDtypeStruct((t, d_rnn), BF16),
        scratch_shapes=[
            pltpu.VMEM((8, bw), F32),
            pltpu.VMEM((8, bw), F32),
            pltpu.VMEM((tt, bw), F32),
            pltpu.VMEM((tt, bw), F32),
            pltpu.VMEM((tt, bw), F32),
        ],
        compiler_params=_params("parallel", "parallel", "arbitrary"),
        name="lru_core",
    )(xb, gb, conv_w, conv_b.reshape(1, -1), ga_w, ga_b.reshape(1, -1), gx_w, gx_b.reshape(1, -1),
      lam.reshape(1, -1))


def kernel(x, c, ada_w, ada_b, ln_g, ln_b, ffn_w_in, ffn_w_out, attn_w_qkv, attn_w_o, attn_lambda_q1, attn_lambda_k1, attn_lambda_q2, attn_lambda_k2, attn_subln_g, lru_w_in, lru_conv_w, lru_conv_b, lru_gate_a_w, lru_gate_a_b, lru_gate_x_w, lru_gate_x_b, lru_lambda, lru_w_out):
    batch, seq, d = x.shape
    depth = ada_w.shape[0]
    heads = d // (2 * DIFF_HEAD_DIM)
    alpha = (2 * depth) ** 0.25
    n_mixers = 2

    mod = _adaln(c, ada_w, ada_b)
    xt = x.reshape(batch * seq, d)
    q_scale = jnp.concatenate([jnp.full((1, d), DIFF_HEAD_DIM ** -0.5 * math.log2(math.e), F32),
                               jnp.ones((1, 2 * d), F32)], axis=1)
    ffn_in = ffn_w_in.astype(BF16).reshape((depth * 2,) + ffn_w_in.shape[2:])
    ffn_out = ffn_w_out.astype(BF16).reshape((depth * 2,) + ffn_w_out.shape[2:])

    for i in range(depth):
        mod3 = mod[i].reshape(batch * SUBLAYERS, MOD_ROWS, d)
        ln = jnp.stack([ln_g[i], ln_b[i]], axis=1)
        xt = _ffn(xt, mod3, 0, ffn_in, ffn_out, 2 * i, ln[0], seq=seq, alpha=alpha)
        j = i // n_mixers
        if i % n_mixers == 0:
            lambda_init = 0.8 - 0.6 * math.exp(-0.3 * i)
            qkv = _qkv_proj(xt, mod3, 1, attn_w_qkv[j].astype(BF16), q_scale, seq=seq)
            mixed = _diff_attention(qkv, attn_lambda_q1[j], attn_lambda_k1[j], attn_lambda_q2[j],
                                    attn_lambda_k2[j], attn_subln_g[j], batch=batch, seq=seq,
                                    heads=heads, lambda_init=lambda_init)
            w_o = attn_w_o[j].astype(BF16)
        else:
            gb, xb = _lru_in_proj(xt, mod3, 1, lru_w_in[j].astype(BF16), seq=seq)
            mixed = _lru_core(xb, gb, lru_conv_w[j], lru_conv_b[j], lru_gate_a_w[j].astype(BF16),
                              lru_gate_a_b[j], lru_gate_x_w[j].astype(BF16), lru_gate_x_b[j],
                              lru_lambda[j], batch=batch, seq=seq)
            w_o = lru_w_out[j].astype(BF16)
        xt = _outproj(mixed, w_o, xt, mod3, 1, ln[1], seq=seq, alpha=alpha)
        xt = _ffn(xt, mod3, 2, ffn_in, ffn_out, 2 * i + 1, ln[2], seq=seq, alpha=alpha)
    return xt.reshape(batch, seq, d)
```

```python
import functools
import math

import jax
import jax.numpy as jnp
from jax import lax
from jax.experimental import pallas as pl
from jax.experimental.pallas import tpu as pltpu

F32 = jnp.float32
BF16 = jnp.bfloat16

CHUNK = 64
DIFF_HEAD_DIM = 128
CONV_WIDTH = 4
LRU_BLOCK_W = 256
LRU_C = 8.0
LN_EPS = 1e-5
RMS_EPS = 1e-5
SUBLAYERS = 3
MOD_ROWS = 3
SHIFT, SCALE, GATE = 0, 1, 2
NEG_BIG = -1e30
FFN_FIRST_ROWS = 512
FFN_LAST_ROWS = 256
PROJ_FIRST_ROWS = 512
OUTPROJ_ROW_BLOCK = 256
ATTN_ROW_BLOCK = 64
ATTN_STEPS_PER_TRIP = 4

VMEM_LIMIT_BYTES = 56 * 1024 * 1024


def _params(*sem):
    return pltpu.CompilerParams(dimension_semantics=sem, vmem_limit_bytes=VMEM_LIMIT_BYTES)


def _dot(a, b):
    return jnp.dot(a, b, preferred_element_type=F32)


def _modulate(x, shift_row, scale_row):
    return x * (1.0 + scale_row) + shift_row


def _residual_layer_norm(x, y, gate_row, res_w, alpha, ln_g, ln_b):
    z = alpha * x + (res_w * (1.0 + gate_row)) * y
    mu = jnp.mean(z, axis=-1, keepdims=True)
    zc = z - mu
    var = jnp.mean(zc * zc, axis=-1, keepdims=True)
    return zc * lax.rsqrt(var + LN_EPS) * ln_g + ln_b


def _adaln_kernel(c_ref, w_ref, b_ref, o_ref):
    c = c_ref[...]
    c_act = (c * jax.nn.sigmoid(c)).astype(BF16)
    o_ref[...] = _dot(c_act, w_ref[...].astype(BF16)) + b_ref[...]


def _adaln(c, ada_w, ada_b, *, tn=1024):
    depth, d, n = ada_w.shape
    b = c.shape[0]
    return pl.pallas_call(
        _adaln_kernel,
        grid=(depth, n // tn),
        in_specs=[
            pl.BlockSpec((b, d), lambda l, j: (0, 0)),
            pl.BlockSpec((None, d, tn), lambda l, j: (l, 0, j)),
            pl.BlockSpec((None, 1, tn), lambda l, j: (l, 0, j)),
        ],
        out_specs=pl.BlockSpec((None, b, tn), lambda l, j: (l, 0, j)),
        out_shape=jax.ShapeDtypeStruct((depth, b, n), F32),
        compiler_params=_params("parallel", "parallel"),
        name="adaln",
    )(c, ada_w, ada_b.reshape(depth, 1, n))


def _mod_spec(sub, tiles_per_batch, d):
    return pl.BlockSpec((None, MOD_ROWS, d),
                        lambda i, *_: ((i // tiles_per_batch) * SUBLAYERS + sub, 0, 0))


def _ln_spec(d):
    return pl.BlockSpec((2, d), lambda *_: (0, 0))


def _row_slices(n_rows, block):
    return [slice(r0, r0 + block) for r0 in range(0, n_rows, block)]


def _modulate_rows(x_ref, md_ref, h_ref, rows):
    h = _modulate(x_ref[rows, :], md_ref[SHIFT:SHIFT + 1, :], md_ref[SCALE:SCALE + 1, :])
    h = h.astype(BF16)
    h_ref[rows, :] = h
    return h


def _ffn_kernel(x_ref, md_ref, wa_ref, wu_ref, wo_ref, ln_ref, o_ref, h_ref, *, alpha):
    j = pl.program_id(1)
    last = pl.num_programs(1) - 1

    def swiglu_chunk(h):
        a = _dot(h, wa_ref[...])
        u = _dot(h, wu_ref[...])
        g = (a * jax.nn.sigmoid(a) * u).astype(BF16)
        return _dot(g, wo_ref[...])

    @pl.when(j == 0)
    def _():
        for rows in _row_slices(x_ref.shape[0], FFN_FIRST_ROWS):
            o_ref[rows, :] = swiglu_chunk(_modulate_rows(x_ref, md_ref, h_ref, rows))

    @pl.when((j > 0) & (j < last))
    def _():
        o_ref[...] += swiglu_chunk(h_ref[...])

    @pl.when(j == last)
    def _():
        for rows in _row_slices(x_ref.shape[0], FFN_LAST_ROWS):
            y = o_ref[rows, :] + swiglu_chunk(h_ref[rows, :])
            o_ref[rows, :] = _residual_layer_norm(x_ref[rows, :], y, md_ref[GATE:GATE + 1, :], 0.5,
                                                  alpha, ln_ref[0:1, :], ln_ref[1:2, :])


def _ffn(x, mod3, sub, w_in, w_out, widx, ln, *, seq, alpha, tm=1024, tf=512):
    t, d = x.shape
    d_ff = w_out.shape[1]
    nf = d_ff // tf
    tpb = seq // tm
    return pl.pallas_call(
        functools.partial(_ffn_kernel, alpha=alpha),
        grid=(t // tm, nf),
        in_specs=[
            pl.BlockSpec((tm, d), lambda i, j: (i, 0)),
            _mod_spec(sub, tpb, d),
            pl.BlockSpec((None, d, tf), lambda i, j: (widx, 0, j)),
            pl.BlockSpec((None, d, tf), lambda i, j: (widx, 0, j + nf)),
            pl.BlockSpec((None, tf, d), lambda i, j: (widx, j, 0)),
            _ln_spec(d),
        ],
        out_specs=pl.BlockSpec((tm, d), lambda i, j: (i, 0)),
        out_shape=jax.ShapeDtypeStruct((t, d), F32),
        scratch_shapes=[pltpu.VMEM((tm, d), BF16)],
        compiler_params=_params("parallel", "arbitrary"),
        name="ffn",
    )(x, mod3, w_in, w_in, w_out, ln)


def _gelu_tanh(x):
    return 0.5 * x * (1.0 + jnp.tanh(math.sqrt(2.0 / math.pi) * (x + 0.044715 * (x * x * x))))


def _qkv_kernel(x_ref, md_ref, w_ref, cs_ref, o_ref, h_ref):
    j = pl.program_id(1)

    @pl.when(j == 0)
    def _():
        for rows in _row_slices(x_ref.shape[0], PROJ_FIRST_ROWS):
            h = _modulate_rows(x_ref, md_ref, h_ref, rows)
            o_ref[rows, :] = (_dot(h, w_ref[...]) * cs_ref[...]).astype(o_ref.dtype)

    @pl.when(j > 0)
    def _():
        o_ref[...] = (_dot(h_ref[...], w_ref[...]) * cs_ref[...]).astype(o_ref.dtype)


def _qkv_proj(x, mod3, sub, w, col_scale, *, seq, tm=1024, tn=1536):
    t, d = x.shape
    n_out = w.shape[1]
    tpb = seq // tm
    return pl.pallas_call(
        _qkv_kernel,
        grid=(t // tm, n_out // tn),
        in_specs=[
            pl.BlockSpec((tm, d), lambda i, j: (i, 0)),
            _mod_spec(sub, tpb, d),
            pl.BlockSpec((d, tn), lambda i, j: (0, j)),
            pl.BlockSpec((1, tn), lambda i, j: (0, j)),
        ],
        out_specs=pl.BlockSpec((tm, tn), lambda i, j: (i, j)),
        out_shape=jax.ShapeDtypeStruct((t, n_out), BF16),
        scratch_shapes=[pltpu.VMEM((tm, d), BF16)],
        compiler_params=_params("parallel", "arbitrary"),
        name="qkv_proj",
    )(x, mod3, w, col_scale)


def _lru_in_kernel(x_ref, md_ref, w_ref, og_ref, ox_ref, h_ref, *, n_gate_tiles):
    j = pl.program_id(1)

    @pl.when(j == 0)
    def _():
        for rows in _row_slices(x_ref.shape[0], PROJ_FIRST_ROWS):
            h = _modulate_rows(x_ref, md_ref, h_ref, rows)
            og_ref[rows, :] = _gelu_tanh(_dot(h, w_ref[...])).astype(og_ref.dtype)

    @pl.when((j > 0) & (j < n_gate_tiles))
    def _():
        og_ref[...] = _gelu_tanh(_dot(h_ref[...], w_ref[...])).astype(og_ref.dtype)

    @pl.when(j >= n_gate_tiles)
    def _():
        ox_ref[...] = _dot(h_ref[...], w_ref[...])


def _lru_in_proj(x, mod3, sub, w, *, seq, tm=1024, tn=1280):
    t, d = x.shape
    d_rnn = w.shape[1] // 2
    tpb = seq // tm
    ng = d_rnn // tn
    return pl.pallas_call(
        functools.partial(_lru_in_kernel, n_gate_tiles=ng),
        grid=(t // tm, 2 * ng),
        in_specs=[
            pl.BlockSpec((tm, d), lambda i, j: (i, 0)),
            _mod_spec(sub, tpb, d),
            pl.BlockSpec((d, tn), lambda i, j: (0, j)),
        ],
        out_specs=[pl.BlockSpec((tm, tn), lambda i, j: (i, jnp.minimum(j, ng - 1))),
                   pl.BlockSpec((tm, tn), lambda i, j: (i, jnp.maximum(j - ng, 0)))],
        out_shape=[jax.ShapeDtypeStruct((t, d_rnn), BF16), jax.ShapeDtypeStruct((t, d_rnn), F32)],
        scratch_shapes=[pltpu.VMEM((tm, d), BF16)],
        compiler_params=_params("parallel", "arbitrary"),
        name="lru_in_proj",
    )(x, mod3, w)


def _outproj_kernel(a_ref, w_ref, x_ref, md_ref, ln_ref, o_ref, *, alpha):
    for r0 in range(0, a_ref.shape[0], OUTPROJ_ROW_BLOCK):
        rows = slice(r0, r0 + OUTPROJ_ROW_BLOCK)
        y = _dot(a_ref[rows, :], w_ref[...])
        o_ref[rows, :] = _residual_layer_norm(x_ref[rows, :], y, md_ref[GATE:GATE + 1, :], 1.0,
                                              alpha, ln_ref[0:1, :], ln_ref[1:2, :])


def _outproj(a, w, x, mod3, sub, ln, *, seq, alpha, tm=512):
    t, d = x.shape
    k = a.shape[1]
    tpb = seq // tm
    return pl.pallas_call(
        functools.partial(_outproj_kernel, alpha=alpha),
        grid=(t // tm,),
        in_specs=[
            pl.BlockSpec((tm, k), lambda i: (i, 0)),
            pl.BlockSpec((k, d), lambda i: (0, 0), pipeline_mode=pl.Buffered(1)),
            pl.BlockSpec((tm, d), lambda i: (i, 0)),
            _mod_spec(sub, tpb, d),
            _ln_spec(d),
        ],
        out_specs=pl.BlockSpec((tm, d), lambda i: (i, 0)),
        out_shape=jax.ShapeDtypeStruct((t, d), F32),
        compiler_params=_params("parallel"),
        name="outproj",
    )(a, w, x, mod3, ln)


def _attn_kernel(lq1_ref, lk1_ref, lq2_ref, lk2_ref, sg_ref, q_ref, k_ref, v_ref, o_ref,
                 s0_ref, rm0_ref, m_ref, l_ref, acc_ref, e_ref, bias_ref, *, tq, lambda_init):
    dh = DIFF_HEAD_DIM
    lanes = l_ref.shape[-1]
    seq = q_ref.shape[0]
    nq = seq // tq
    n_pairs = nq * (nq + 1) // 2
    assert n_pairs % ATTN_STEPS_PER_TRIP == 0 and tq // CHUNK <= lanes
    nt = (((1,), (1,)), ((), ()))

    r = lax.broadcasted_iota(jnp.int32, (tq, lanes), 0) // CHUNK
    c = lax.broadcasted_iota(jnp.int32, (tq, lanes), 1)
    e_ref[...] = jnp.where(r == c, 1.0, 0.0).astype(BF16)
    bias_ref[0] = jnp.zeros((tq, lanes), BF16)
    bias_ref[1] = jnp.where((r <= c) | (c >= tq // CHUNK), 0.0, NEG_BIG).astype(BF16)
    m_ref[...] = jnp.full(m_ref.shape, NEG_BIG, F32)
    l_ref[...] = jnp.zeros(l_ref.shape, F32)
    acc_ref[...] = jnp.zeros(acc_ref.shape, F32)

    def scores(qi, kt, diagonal):
        q0 = pl.multiple_of(qi * tq, tq)
        k0 = pl.multiple_of(kt * tq, tq)
        out = []
        for s in range(2):
            qa = q_ref[pl.ds(q0, tq), s * dh:(s + 1) * dh]
            ka = k_ref[pl.ds(k0, tq), s * dh:(s + 1) * dh]
            if diagonal:
                qa = jnp.concatenate([qa, e_ref[...]], axis=1)
                ka = jnp.concatenate([ka, bias_ref[1]], axis=1)
            sc = lax.dot_general(qa, ka, nt, preferred_element_type=F32)
            mx = functools.reduce(jnp.maximum,
                                  [sc[:, j * lanes:(j + 1) * lanes] for j in range(tq // lanes)])
            out.append((sc, jnp.broadcast_to(jnp.max(mx, axis=-1, keepdims=True), (tq, lanes))))
        return out

    def fold(qi, kt, cur):
        q0 = pl.multiple_of(qi * tq, tq)
        v = v_ref[pl.ds(pl.multiple_of(kt * tq, tq), tq), :]
        for s in range(2):
            s_in, rm_in = cur[s]
            p_blocks, c_blocks = [], []
            for r0 in range(0, tq, ATTN_ROW_BLOCK):
                rows = slice(r0, r0 + ATTN_ROW_BLOCK)
                srows = pl.ds(pl.multiple_of(q0 + r0, ATTN_ROW_BLOCK), ATTN_ROW_BLOCK)
                sc = [s_in[rows, j * lanes:(j + 1) * lanes] for j in range(tq // lanes)]
                m_old = m_ref[s, srows, :]
                m_new = jnp.maximum(m_old, rm_in[rows, :])
                corr = jnp.exp2(m_old - m_new)
                p = [jnp.exp2(x - m_new) for x in sc]
                l_ref[s, srows, :] = corr * l_ref[s, srows, :] + functools.reduce(jnp.add, p)
                m_ref[s, srows, :] = m_new
                c_blocks.append(corr)
                p_blocks.append(jnp.concatenate(p, axis=1).astype(BF16))
            pv = _dot(jnp.concatenate(p_blocks, axis=0), v)
            corr = jnp.concatenate(c_blocks, axis=0)
            qrows = pl.ds(q0, tq)
            acc_ref[s, qrows, :] = jnp.concatenate([corr, corr], axis=1) * acc_ref[s, qrows, :] + pv

    def stash(cur):
        for s in range(2):
            s0_ref[s], rm0_ref[s] = cur[s]

    def make_trip(diagonal):
        def trip(_, carry):
            qi, kt = carry
            cur = [(s0_ref[s], rm0_ref[s]) for s in range(2)]
            for _ in range(ATTN_STEPS_PER_TRIP):
                if diagonal:
                    nqi = jnp.minimum(qi + 1, nq - 1)
                    nkt = nqi
                else:
                    last = kt == qi - 1
                    nqi = jnp.minimum(jnp.where(last, qi + 1, qi), nq - 1)
                    nkt = jnp.where(last, 0, kt + 1)
                nxt = scores(nqi, nkt, diagonal)
                fold(qi, kt, cur)
                cur, qi, kt = nxt, nqi, nkt
            stash(cur)
            return qi, kt
        return trip

    n_off = n_pairs - nq
    assert n_off % ATTN_STEPS_PER_TRIP == 0 and nq % ATTN_STEPS_PER_TRIP == 0
    stash(scores(1, 0, False))
    lax.fori_loop(0, n_off // ATTN_STEPS_PER_TRIP, make_trip(False), (jnp.int32(1), jnp.int32(0)))
    stash(scores(0, 0, True))
    lax.fori_loop(0, nq // ATTN_STEPS_PER_TRIP, make_trip(True), (jnp.int32(0), jnp.int32(0)))

    lam = (jnp.exp(jnp.sum(lq1_ref[...] * lk1_ref[...], axis=-1, keepdims=True))
           - jnp.exp(jnp.sum(lq2_ref[...] * lk2_ref[...], axis=-1, keepdims=True))
           + lambda_init)
    out_scale = sg_ref[...] * (1.0 - lambda_init)

    def normalise(i, carry):
        rows = pl.ds(pl.multiple_of(i * ATTN_ROW_BLOCK, ATTN_ROW_BLOCK), ATTN_ROW_BLOCK)
        inv1 = 1.0 / jnp.sum(l_ref[0, rows, :], axis=-1, keepdims=True)
        inv2 = lam / jnp.sum(l_ref[1, rows, :], axis=-1, keepdims=True)
        o = acc_ref[0, rows, :] * inv1 - acc_ref[1, rows, :] * inv2
        o = o * lax.rsqrt(jnp.mean(o * o, axis=-1, keepdims=True) + RMS_EPS)
        o_ref[rows, :] = (o * out_scale).astype(o_ref.dtype)
        return carry

    lax.fori_loop(0, seq // ATTN_ROW_BLOCK, normalise, 0, unroll=8)


def _diff_attention(qkv, lq1, lk1, lq2, lk2, subln_g, *, batch, seq, heads, lambda_init, tq=512):
    dv = 2 * DIFF_HEAD_DIM
    lanes = 128
    vec = pl.BlockSpec((1, DIFF_HEAD_DIM), lambda b, h: (0, 0))
    return pl.pallas_call(
        functools.partial(_attn_kernel, tq=tq, lambda_init=lambda_init),
        grid=(batch, heads),
        in_specs=[
            vec, vec, vec, vec,
            pl.BlockSpec((1, dv), lambda b, h: (0, 0)),
            pl.BlockSpec((seq, dv), lambda b, h: (b, h)),
            pl.BlockSpec((seq, dv), lambda b, h: (b, heads + h)),
            pl.BlockSpec((seq, dv), lambda b, h: (b, 2 * heads + h)),
        ],
        out_specs=pl.BlockSpec((seq, dv), lambda b, h: (b, h)),
        out_shape=jax.ShapeDtypeStruct((batch * seq, heads * dv), BF16),
        scratch_shapes=[
            pltpu.VMEM((2, tq, tq), F32), pltpu.VMEM((2, tq, lanes), F32),
            pltpu.VMEM((2, seq, lanes), F32), pltpu.VMEM((2, seq, lanes), F32),
            pltpu.VMEM((2, seq, dv), F32),
            pltpu.VMEM((tq, lanes), BF16), pltpu.VMEM((2, tq, lanes), BF16),
        ],
        compiler_params=_params("parallel", "parallel"),
        name="diff_attn",
    )(lq1.reshape(1, -1), lk1.reshape(1, -1), lq2.reshape(1, -1), lk2.reshape(1, -1),
      subln_g.reshape(1, dv), qkv, qkv, qkv)


def _lru_kernel(x_ref, g_ref, cw_ref, cb_ref, gaw_ref, gab_ref, gxw_ref, gxb_ref, lam_ref,
                o_ref, xe_ref, hc_ref, a_s, b_s, h_s, *, tt):
    t = pl.program_id(2)
    w = x_ref.shape[1]

    @pl.when(t == 0)
    def _():
        xe_ref[...] = jnp.zeros_like(xe_ref)
        hc_ref[...] = jnp.zeros_like(hc_ref)

    ng = tt // 8
    x3 = x_ref[...].reshape(ng, 8, w)
    xall = jnp.concatenate([xe_ref[...][None], x3], axis=0)
    xe_ref[...] = x3[ng - 1]
    row = lax.broadcasted_iota(jnp.int32, (ng, 8, w), 1)
    cw = cw_ref[...]
    xc = cw[CONV_WIDTH - 1:CONV_WIDTH, :] * x3 + cb_ref[...]
    for s in range(1, CONV_WIDTH):
        rot = pltpu.roll(xall, s, 1)
        shifted = jnp.where(row < s, rot[:ng], rot[1:])
        xc = xc + cw[CONV_WIDTH - 1 - s:CONV_WIDTH - s, :] * shifted

    xcb = xc.reshape(tt, w).astype(BF16)
    r = jax.nn.sigmoid(_dot(xcb, gaw_ref[...]) + gab_ref[...]).reshape(ng, 8, w)
    gi = jax.nn.sigmoid(_dot(xcb, gxw_ref[...]) + gxb_ref[...]).reshape(ng, 8, w)
    z = -lam_ref[...]
    softplus = jnp.maximum(z, 0.0) + jnp.log(1.0 + jnp.exp(-jnp.abs(z)))
    a = jnp.exp2(r * ((-LRU_C * math.log2(math.e)) * softplus))
    mult = jnp.exp2(0.5 * jnp.log2(jnp.maximum(1.0 - a * a, 0.0)))
    bx = xc * gi * mult

    for d in (1, 2, 4):
        keep = row >= d
        a_sh = jnp.where(keep, pltpu.roll(a, d, 1), 1.0)
        b_sh = jnp.where(keep, pltpu.roll(bx, d, 1), 0.0)
        bx = a * b_sh + bx
        a = a * a_sh
    a_s[...] = a.reshape(tt, w)
    b_s[...] = bx.reshape(tt, w)

    def body(g, carry):
        off = pl.multiple_of(g * 8, 8)
        h = a_s[pl.ds(off, 8), :] * carry + b_s[pl.ds(off, 8), :]
        h_s[pl.ds(off, 8), :] = h
        return h[7:8, :]

    carry = lax.fori_loop(0, tt // 8, body, hc_ref[0:1, :], unroll=8)
    hc_ref[0:1, :] = carry
    o_ref[...] = (h_s[...] * g_ref[...].astype(F32)).astype(o_ref.dtype)


def _lru_core(xb, gb, conv_w, conv_b, ga_w, ga_b, gx_w, gx_b, lam, *, batch, seq, tt=2048):
    t, d_rnn = xb.shape
    bw = LRU_BLOCK_W
    nb = d_rnn // bw
    nt = seq // tt
    tile = lambda b, n, s: (b * nt + s, n)
    row = lambda b, n, s: (0, n)
    gate_w = lambda b, n, s: (n, 0, 0)
    return pl.pallas_call(
        functools.partial(_lru_kernel, tt=tt),
        grid=(batch, nb, nt),
        in_specs=[
            pl.BlockSpec((tt, bw), tile),
            pl.BlockSpec((tt, bw), tile),
            pl.BlockSpec((CONV_WIDTH, bw), row),
            pl.BlockSpec((1, bw), row),
            pl.BlockSpec((None, bw, bw), gate_w),
            pl.BlockSpec((1, bw), row),
            pl.BlockSpec((None, bw, bw), gate_w),
            pl.BlockSpec((1, bw), row),
            pl.BlockSpec((1, bw), row),
        ],
        out_specs=pl.BlockSpec((tt, bw), tile),
        out_shape=jax.ShapeDtypeStruct((t, d_rnn), BF16),
        scratch_shapes=[
            pltpu.VMEM((8, bw), F32),
            pltpu.VMEM((8, bw), F32),
            pltpu.VMEM((tt, bw), F32),
            pltpu.VMEM((tt, bw), F32),
            pltpu.VMEM((tt, bw), F32),
        ],
        compiler_params=_params("parallel", "parallel", "arbitrary"),
        name="lru_core",
    )(xb, gb, conv_w, conv_b.reshape(1, -1), ga_w, ga_b.reshape(1, -1), gx_w, gx_b.reshape(1, -1),
      lam.reshape(1, -1))


def kernel(x, c, ada_w, ada_b, ln_g, ln_b, ffn_w_in, ffn_w_out, attn_w_qkv, attn_w_o, attn_lambda_q1, attn_lambda_k1, attn_lambda_q2, attn_lambda_k2, attn_subln_g, lru_w_in, lru_conv_w, lru_conv_b, lru_gate_a_w, lru_gate_a_b, lru_gate_x_w, lru_gate_x_b, lru_lambda, lru_w_out):
    batch, seq, d = x.shape
    depth = ada_w.shape[0]
    heads = d // (2 * DIFF_HEAD_DIM)
    alpha = (2 * depth) ** 0.25
    n_mixers = 2

    mod = _adaln(c, ada_w, ada_b)
    xt = x.reshape(batch * seq, d)
    q_scale = jnp.concatenate([jnp.full((1, d), DIFF_HEAD_DIM ** -0.5 * math.log2(math.e), F32),
                               jnp.ones((1, 2 * d), F32)], axis=1)
    ffn_in = ffn_w_in.astype(BF16).reshape((depth * 2,) + ffn_w_in.shape[2:])
    ffn_out = ffn_w_out.astype(BF16).reshape((depth * 2,) + ffn_w_out.shape[2:])

    for i in range(depth):
        mod3 = mod[i].reshape(batch * SUBLAYERS, MOD_ROWS, d)
        ln = jnp.stack([ln_g[i], ln_b[i]], axis=1)
        xt = _ffn(xt, mod3, 0, ffn_in, ffn_out, 2 * i, ln[0], seq=seq, alpha=alpha)
        j = i // n_mixers
        if i % n_mixers == 0:
            lambda_init = 0.8 - 0.6 * math.exp(-0.3 * i)
            qkv = _qkv_proj(xt, mod3, 1, attn_w_qkv[j].astype(BF16), q_scale, seq=seq)
            mixed = _diff_attention(qkv, attn_lambda_q1[j], attn_lambda_k1[j], attn_lambda_q2[j],
                                    attn_lambda_k2[j], attn_subln_g[j], batch=batch, seq=seq,
                                    heads=heads, lambda_init=lambda_init)
            w_o = attn_w_o[j].astype(BF16)
        else:
            gb, xb = _lru_in_proj(xt, mod3, 1, lru_w_in[j].astype(BF16), seq=seq)
            mixed = _lru_core(xb, gb, lru_conv_w[j], lru_conv_b[j], lru_gate_a_w[j].astype(BF16),
                              lru_gate_a_b[j], lru_gate_x_w[j].astype(BF16), lru_gate_x_b[j],
                              lru_lambda[j], batch=batch, seq=seq)
            w_o = lru_w_out[j].astype(BF16)
        xt = _outproj(mixed, w_o, xt, mod3, 1, ln[1], seq=seq, alpha=alpha)
        xt = _ffn(xt, mod3, 2, ffn_in, ffn_out, 2 * i + 1, ln[2], seq=seq, alpha=alpha)
    return xt.reshape(batch, seq, d)
```
